```python
import math
import numpy as np
import jax, jax.numpy as jnp
from jax import lax

D_MODEL = 1024
BATCH = 16
SEQ = 2048
DEPTH = 4

ROPE_THETA = 10000.0
NORM_EPS = 1e-6
Q_BLOCK = 128

A_HEADS = 8
A_NOPE = 64
A_ROPE = 32
A_KV_RANK = 128
A_V_DIM = 64
A_SCALE = (A_NOPE + A_ROPE) ** -0.5
IDX_HEADS = 8
IDX_DIM = 64
TOPK_MAX = 256

B_HEADS = 4
B_QK_DIM = 64
B_V_DIM = 2 * B_QK_DIM

C_HEADS = 16
C_GROUPS = 4
C_HPG = C_HEADS // C_GROUPS
C_DIM = 64
CMP_LEN = 32
CMP_STRIDE = 16
CMP_HIDDEN = 128
SEL_BLOCK = 64
SEL_TOPN = 16
WINDOW = 512
C_Q_BLOCK = 16

D_FF = -(-8 * D_MODEL // (3 * 256)) * 256

EVEN_SPLITS = [A_HEADS * A_NOPE, A_HEADS * A_ROPE, A_KV_RANK, A_ROPE, IDX_HEADS * IDX_DIM, IDX_DIM, IDX_HEADS,
               B_HEADS * 2 * B_QK_DIM, B_HEADS * 2 * B_QK_DIM, B_HEADS * B_V_DIM]
EVEN_IN = sum(EVEN_SPLITS)
ODD_SPLITS = [C_HEADS * C_DIM] + [C_GROUPS * C_DIM] * 6 + [C_HEADS * 3]
ODD_IN = sum(ODD_SPLITS)

kernel_name = 'hybrid_dsa_diff_nsa_trunk'


def rmsnorm(x, g):
    xf = x.astype(jnp.float32)
    y = xf * lax.rsqrt(jnp.mean(xf * xf, axis=-1, keepdims=True) + NORM_EPS)
    return (y * g.astype(jnp.float32)).astype(x.dtype)


def rope(x, pos):
    d = x.shape[-1]
    inv = ROPE_THETA ** (-jnp.arange(0, d, 2, dtype=jnp.float32) / d)
    ang = pos.astype(jnp.float32)[:, None] * inv[None, :]
    shape = (x.shape[1],) + (1,) * (x.ndim - 3) + (d // 2,)
    cos = jnp.cos(ang).reshape(shape)
    sin = jnp.sin(ang).reshape(shape)
    xf = x.astype(jnp.float32)
    x1, x2 = xf[..., :d // 2], xf[..., d // 2:]
    return jnp.concatenate([x1 * cos - x2 * sin, x2 * cos + x1 * sin], axis=-1).astype(x.dtype)


def safe_softmax(s, mask):
    s = jnp.where(mask, s, -jnp.inf)
    m = jnp.max(s, axis=-1, keepdims=True)
    m = jnp.where(jnp.isfinite(m), m, 0.0)
    e = jnp.exp(s - m)
    den = jnp.sum(e, axis=-1, keepdims=True)
    return e / jnp.where(den > 0, den, 1.0)


def split_cols(y, sizes):
    return jnp.split(y, np.cumsum(sizes)[:-1].tolist(), axis=-1)


def qslice(a, i, qb):
    return lax.dynamic_slice_in_dim(a, i * qb, qb, axis=1)


def map_blocks(fn, n_blocks):
    out = lax.map(fn, jnp.arange(n_blocks, dtype=jnp.int32))
    nb, b, qb = out.shape[:3]
    return jnp.moveaxis(out, 0, 1).reshape((b, nb * qb) + out.shape[3:])


def dsa_mixer(q_nope, q_rope, c_kv, k_rope, iq, ik, iw, kv_gain, w_uk, w_uv, pos):
    B, T = c_kv.shape[:2]
    q_nope = q_nope.reshape(B, T, A_HEADS, A_NOPE)
    q_rope = rope(q_rope.reshape(B, T, A_HEADS, A_ROPE), pos)
    c_kv = rmsnorm(c_kv, kv_gain)
    k_rope = rope(k_rope, pos)
    q_lat = jnp.einsum('bthd,hdc->bthc', q_nope, w_uk)
    qc = jnp.concatenate([q_lat, q_rope], axis=-1)
    kc = jnp.concatenate([c_kv, k_rope], axis=-1)
    iq = rope(iq.reshape(B, T, IDX_HEADS, IDX_DIM), pos)
    ik = rope(ik, pos)
    iw = iw * IDX_HEADS ** -0.5
    k_sel = min(TOPK_MAX, T // 4)

    def block(i):
        t = i * Q_BLOCK + jnp.arange(Q_BLOCK, dtype=jnp.int32)
        rel = jax.nn.relu(jnp.einsum('bqhd,bsd->bqhs', qslice(iq, i, Q_BLOCK), ik).astype(jnp.float32) * IDX_DIM ** -0.5)
        score = jnp.einsum('bqhs,bqh->bqs', rel, qslice(iw, i, Q_BLOCK).astype(jnp.float32))
        causal = pos[None, :] <= t[:, None]
        score = jnp.where(causal[None], score, -jnp.inf)
        _, idx = lax.top_k(score, k_sel)
        kg = jax.vmap(lambda k, j: k[j])(kc, idx)
        s = jnp.einsum('bqhc,bqkc->bqhk', qslice(qc, i, Q_BLOCK), kg).astype(jnp.float32) * A_SCALE
        valid = (idx <= t[None, :, None])[:, :, None, :]
        p = safe_softmax(s, valid)
        return jnp.einsum('bqhk,bqkc->bqhc', p.astype(kg.dtype), kg[..., :A_KV_RANK])

    o_lat = map_blocks(block, T // Q_BLOCK)
    o = jnp.einsum('bthc,hcd->bthd', o_lat, w_uv)
    return o.reshape(B, T, A_HEADS * A_V_DIM)


def diff_mixer(q, k, v, lam, subln_gain, lam_init, pos):
    B, T = q.shape[:2]
    q = rope(q.reshape(B, T, B_HEADS * 2, B_QK_DIM), pos).reshape(B, T, B_HEADS, 2, B_QK_DIM)
    k = rope(k.reshape(B, T, B_HEADS * 2, B_QK_DIM), pos).reshape(B, T, B_HEADS, 2, B_QK_DIM)
    v = v.reshape(B, T, B_HEADS, B_V_DIM)
    lf = lam.astype(jnp.float32)
    lam_full = jnp.exp(jnp.sum(lf[0] * lf[1])) - jnp.exp(jnp.sum(lf[2] * lf[3])) + lam_init

    def block(i):
        t = i * Q_BLOCK + jnp.arange(Q_BLOCK, dtype=jnp.int32)
        s = jnp.einsum('bqhmd,bshmd->bhmqs', qslice(q, i, Q_BLOCK), k).astype(jnp.float32) * B_QK_DIM ** -0.5
        causal = pos[None, :] <= t[:, None]
        p = safe_softmax(s, causal)
        a = p[:, :, 0] - lam_full * p[:, :, 1]
        return jnp.einsum('bhqs,bshe->bqhe', a.astype(v.dtype), v)

    o = map_blocks(block, T // Q_BLOCK)
    o = rmsnorm(o, subln_gain) * (1.0 - lam_init)
    return o.reshape(B, T, B_HEADS * B_V_DIM)


def compress(x, pe, w1, w2):
    B, T, G, d = x.shape
    chunks = x.reshape(B, T // CMP_STRIDE, CMP_STRIDE, G, d)
    blocks = jnp.concatenate([chunks[:, :-1], chunks[:, 1:]], axis=2) + pe[None, None, :, None, :]
    flat = blocks.transpose(0, 1, 3, 2, 4).reshape(B, -1, G, CMP_LEN * d)
    return jnp.einsum('bngf,fe->bnge', jax.nn.gelu(flat @ w1), w2)


def nsa_mixer(q, kc_raw, vc_raw, ks, vs, kw, vw, gates, pe, w1, w2, pos):
    B, T = q.shape[:2]
    kv = lambda a: a.reshape(B, T, C_GROUPS, C_DIM)
    q = rope(q.reshape(B, T, C_HEADS, C_DIM), pos).reshape(B, T, C_GROUPS, C_HPG, C_DIM)
    k_cmp = compress(rope(kv(kc_raw), pos), pe[0], w1[0], w2[0])
    v_cmp = compress(kv(vc_raw), pe[1], w1[1], w2[1])
    k_slc, v_slc = rope(kv(ks), pos), kv(vs)
    k_win, v_win = rope(kv(kw), pos), kv(vw)
    gates = jax.nn.sigmoid(gates.astype(jnp.float32)).reshape(B, T, C_GROUPS, C_HPG, 3).astype(q.dtype)
    n_cmp = T // CMP_STRIDE - 1
    n_blk = T // SEL_BLOCK
    top_n = min(SEL_TOPN, n_blk)
    cmp_end = jnp.arange(n_cmp, dtype=jnp.int32) * CMP_STRIDE + CMP_LEN - 1
    cs = np.arange(n_cmp) * CMP_STRIDE
    ss = np.arange(n_blk) * SEL_BLOCK
    overlap = jnp.asarray(((cs[:, None] < ss[None, :] + SEL_BLOCK) & (cs[:, None] + CMP_LEN > ss[None, :])).astype(np.float32))
    k_blocks = k_slc.reshape(B, n_blk, SEL_BLOCK, C_GROUPS, C_DIM).transpose(0, 3, 1, 2, 4)
    v_blocks = v_slc.reshape(B, n_blk, SEL_BLOCK, C_GROUPS, C_DIM).transpose(0, 3, 1, 2, 4)
    k_pad = jnp.pad(k_win, ((0, 0), (WINDOW, 0), (0, 0), (0, 0)))
    v_pad = jnp.pad(v_win, ((0, 0), (WINDOW, 0), (0, 0), (0, 0)))
    blk_ids = jnp.arange(n_blk, dtype=jnp.int32)
    scale = C_DIM ** -0.5
    gather = jax.vmap(jax.vmap(lambda a, j: a[j]))

    def block(i):
        t = i * C_Q_BLOCK + jnp.arange(C_Q_BLOCK, dtype=jnp.int32)
        qb = qslice(q, i, C_Q_BLOCK)
        s = jnp.einsum('bqghd,bngd->bqghn', qb, k_cmp).astype(jnp.float32) * scale
        p_cmp = safe_softmax(s, (cmp_end[None, :] <= t[:, None])[:, None, None, :])
        o_cmp = jnp.einsum('bqghn,bngd->bqghd', p_cmp.astype(v_cmp.dtype), v_cmp)
        imp = jnp.einsum('bqghn,nj->bqgj', p_cmp, overlap)
        cur = t // SEL_BLOCK
        forced = (blk_ids[None, :] == 0) | (blk_ids[None, :] == cur[:, None]) | (blk_ids[None, :] == cur[:, None] - 1)
        admissible = blk_ids[None, :] * SEL_BLOCK <= t[:, None]
        imp = jnp.where(forced[:, None, :], jnp.inf, imp)
        imp = jnp.where(admissible[:, None, :], imp, -jnp.inf)
        _, sel = lax.top_k(imp, top_n)
        sel = sel.transpose(0, 2, 1, 3)
        kg = gather(k_blocks, sel)
        vg = gather(v_blocks, sel)
        s = jnp.einsum('bqghd,bgqnld->bqghnl', qb, kg).astype(jnp.float32) * scale
        tok = sel[..., None] * SEL_BLOCK + jnp.arange(SEL_BLOCK, dtype=jnp.int32)
        m_slc = (tok <= t[None, None, :, None, None]).transpose(0, 2, 1, 3, 4)
        m_slc = m_slc.reshape(B, C_Q_BLOCK, C_GROUPS, 1, top_n * SEL_BLOCK)
        p = safe_softmax(s.reshape(B, C_Q_BLOCK, C_GROUPS, C_HPG, top_n * SEL_BLOCK), m_slc)
        p = p.reshape(B, C_Q_BLOCK, C_GROUPS, C_HPG, top_n, SEL_BLOCK)
        o_slc = jnp.einsum('bqghnl,bgqnld->bqghd', p.astype(vg.dtype), vg)
        kwb = lax.dynamic_slice_in_dim(k_pad, i * C_Q_BLOCK, WINDOW + C_Q_BLOCK, axis=1)
        vwb = lax.dynamic_slice_in_dim(v_pad, i * C_Q_BLOCK, WINDOW + C_Q_BLOCK, axis=1)
        kpos = i * C_Q_BLOCK - WINDOW + jnp.arange(WINDOW + C_Q_BLOCK, dtype=jnp.int32)
        dist = t[:, None] - kpos[None, :]
        m_win = ((dist >= 0) & (dist < WINDOW) & (kpos[None, :] >= 0))[:, None, None, :]
        s = jnp.einsum('bqghd,bsgd->bqghs', qb, kwb).astype(jnp.float32) * scale
        p = safe_softmax(s, m_win)
        o_win = jnp.einsum('bqghs,bsgd->bqghd', p.astype(vwb.dtype), vwb)
        g = qslice(gates, i, C_Q_BLOCK)
        return g[..., 0:1] * o_cmp + g[..., 1:2] * o_slc + g[..., 2:3] * o_win

    o = map_blocks(block, T // C_Q_BLOCK)
    return o.reshape(B, T, C_HEADS * C_DIM)


def swiglu(h, w_gate, w_up, w_down):
    return (jax.nn.silu(h @ w_gate) * (h @ w_up)) @ w_down


def _w(key, shape, fan_in, gain=1.0):
    return jax.random.normal(key, shape, jnp.float32) * (gain * fan_in ** -0.5)


def _gain(key, shape):
    return 1.0 + 0.02 * jax.random.normal(key, shape, jnp.float32)


def setup_inputs(seed: int = 0) -> dict:
    key = jax.random.key(seed)
    k = jax.random.split(key, 19)
    n_even = (DEPTH + 1) // 2
    n_odd = DEPTH // 2
    out_gain = (2 * DEPTH) ** -0.5
    return {
        'x': jax.random.normal(k[0], (BATCH, SEQ, D_MODEL), jnp.float32),
        'norm_mix': _gain(k[1], (DEPTH, D_MODEL)),
        'norm_ffn': _gain(k[2], (DEPTH, D_MODEL)),
        'norm_final': _gain(k[3], (D_MODEL,)),
        'ev_w_in': _w(k[4], (n_even, D_MODEL, EVEN_IN), D_MODEL),
        'ev_kv_gain': _gain(k[5], (n_even, A_KV_RANK)),
        'ev_w_uk': _w(k[6], (n_even, A_HEADS, A_NOPE, A_KV_RANK), A_KV_RANK),
        'ev_w_uv': _w(k[7], (n_even, A_HEADS, A_KV_RANK, A_V_DIM), A_KV_RANK),
        'ev_lambda': 0.1 * jax.random.normal(k[8], (n_even, 4, B_QK_DIM), jnp.float32),
        'ev_subln': _gain(k[9], (n_even, B_V_DIM)),
        'ev_w_out': _w(k[10], (n_even, D_MODEL, D_MODEL), D_MODEL, out_gain),
        'od_w_in': _w(k[11], (n_odd, D_MODEL, ODD_IN), D_MODEL),
        'od_cmp_pe': 0.1 * jax.random.normal(k[12], (n_odd, 2, CMP_LEN, C_DIM), jnp.float32),
        'od_cmp_w1': _w(k[13], (n_odd, 2, CMP_LEN * C_DIM, CMP_HIDDEN), CMP_LEN * C_DIM),
        'od_cmp_w2': _w(k[14], (n_odd, 2, CMP_HIDDEN, C_DIM), CMP_HIDDEN),
        'od_w_out': _w(k[15], (n_odd, D_MODEL, D_MODEL), D_MODEL, out_gain),
        'ffn_w_gate': _w(k[16], (DEPTH, D_MODEL, D_FF), D_MODEL),
        'ffn_w_up': _w(k[17], (DEPTH, D_MODEL, D_FF), D_MODEL),
        'ffn_w_down': _w(k[18], (DEPTH, D_FF, D_MODEL), D_FF, out_gain),
    }


def reference(x, norm_mix, norm_ffn, norm_final,
              ev_w_in, ev_kv_gain, ev_w_uk, ev_w_uv, ev_lambda, ev_subln, ev_w_out,
              od_w_in, od_cmp_pe, od_cmp_w1, od_cmp_w2, od_w_out,
              ffn_w_gate, ffn_w_up, ffn_w_down):
    pos = jnp.arange(x.shape[1], dtype=jnp.int32)
    for layer in range(DEPTH):
        j = layer // 2
        h = rmsnorm(x, norm_mix[layer])
        if layer % 2 == 0:
            y = h @ ev_w_in[j]
            qa_nope, qa_rope, c_kv, ka_rope, iq, ik, iw, qb, kb, vb = split_cols(y, EVEN_SPLITS)
            o_a = dsa_mixer(qa_nope, qa_rope, c_kv, ka_rope, iq, ik, iw, ev_kv_gain[j], ev_w_uk[j], ev_w_uv[j], pos)
            lam_init = 0.8 - 0.6 * math.exp(-0.3 * layer)
            o_b = diff_mixer(qb, kb, vb, ev_lambda[j], ev_subln[j], lam_init, pos)
            mix = jnp.concatenate([o_a, o_b], axis=-1) @ ev_w_out[j]
        else:
            y = h @ od_w_in[j]
            qc, kc_raw, vc_raw, ks, vs, kw, vw, gc = split_cols(y, ODD_SPLITS)
            o_c = nsa_mixer(qc, kc_raw, vc_raw, ks, vs, kw, vw, gc, od_cmp_pe[j], od_cmp_w1[j], od_cmp_w2[j], pos)
            mix = o_c @ od_w_out[j]
        x = x + mix
        x = x + swiglu(rmsnorm(x, norm_ffn[layer]), ffn_w_gate[layer], ffn_w_up[layer], ffn_w_down[layer])
    return rmsnorm(x, norm_final)
```

```python
import functools
import math

import numpy as np
import jax
import jax.numpy as jnp
from jax import lax
from jax.experimental import pallas as pl
from jax.experimental.pallas import tpu as pltpu

F32 = jnp.float32
BF16 = jnp.bfloat16

D_MODEL = 1024
DEPTH = 4
ROPE_THETA = 10000.0
NORM_EPS = 1e-6

A_HEADS = 8
A_NOPE = 64
A_ROPE = 32
A_KV_RANK = 128
A_V_DIM = 64
A_SCALE = (A_NOPE + A_ROPE) ** -0.5
IDX_HEADS = 8
IDX_DIM = 64
TOPK_MAX = 256

B_HEADS = 4
B_QK_DIM = 64
B_V_DIM = 2 * B_QK_DIM

C_HEADS = 16
C_GROUPS = 4
C_HPG = C_HEADS // C_GROUPS
C_DIM = 64
CMP_LEN = 32
CMP_STRIDE = 16
CMP_HIDDEN = 128
SEL_BLOCK = 64
SEL_TOPN = 16
WINDOW = 512

D_FF = -(-8 * D_MODEL // (3 * 256)) * 256

EVEN_SPLITS = [A_HEADS * A_NOPE, A_HEADS * A_ROPE, A_KV_RANK, A_ROPE, IDX_HEADS * IDX_DIM, IDX_DIM, IDX_HEADS,
               B_HEADS * 2 * B_QK_DIM, B_HEADS * 2 * B_QK_DIM, B_HEADS * B_V_DIM]
ODD_SPLITS = [C_HEADS * C_DIM] + [C_GROUPS * C_DIM] * 6 + [C_HEADS * 3]

LANES = 128
VMEM_LIMIT = 56 * 1024 * 1024
MASKED = -1e30
KEY_NEG_INF = -2139095041
INT_MIN = -2147483648

PROJ_TM = 512
A_QB = 128
A_KC = 256
B_QB = 256
C_QB = 256


def _cparams(sem):
    return pltpu.CompilerParams(dimension_semantics=sem, vmem_limit_bytes=VMEM_LIMIT)


def _dot(a, b):
    return jnp.dot(a, b, preferred_element_type=F32)


def _dot_nt(a, b):
    return lax.dot_general(a, b, (((1,), (1,)), ((), ())), preferred_element_type=F32)


def _rms_rows(x, gain):
    return x * lax.rsqrt(jnp.mean(x * x, axis=-1, keepdims=True) + NORM_EPS) * gain


def _rope_block(y, cos, s_hi, s_lo, half):
    return y * cos + pltpu.roll(y, half, 1) * s_hi + pltpu.roll(y, LANES - half, 1) * s_lo


def _rope_tables(seq, d):
    half = d // 2
    pos = jnp.arange(seq, dtype=jnp.int32)
    inv = ROPE_THETA ** (-jnp.arange(0, d, 2, dtype=F32) / d)
    ang = pos.astype(F32)[:, None] * inv[None, :]
    cos, sin = jnp.cos(ang), jnp.sin(ang)
    lane = np.arange(LANES)
    idx = (lane % d) % half
    first = jnp.asarray((lane % d) < half)
    c = cos[:, idx]
    s = sin[:, idx]
    return jnp.stack([c, jnp.where(first[None, :], 0.0, s), jnp.where(first[None, :], -s, 0.0)])


EV_QN = (0, 512)
EV_R64 = (512, 2176)
EV_R32 = (2176, 3328)
EV_CKV = (3328, 3456)
EV_VB = (3456, 3968)
EV_IW = (3968, 4096)
EV_COLS = 4096


def _even_weight(w_in):
    qa_nope, qa_rope, c_kv, ka_rope, iq, ik, iw, qb, kb, vb = jnp.split(w_in, np.cumsum(EVEN_SPLITS)[:-1].tolist(), axis=1)
    d = w_in.shape[0]
    z = lambda n: jnp.zeros((d, n), w_in.dtype)
    qr = jnp.pad(qa_rope.reshape(d, A_HEADS, A_ROPE), ((0, 0), (0, 0), (0, LANES - A_ROPE))).reshape(d, A_HEADS * LANES)
    cols = [qa_nope, iq, qb, kb, ik, z(LANES - IDX_DIM), qr, ka_rope, z(LANES - A_ROPE), c_kv, vb, iw, z(LANES - IDX_HEADS)]
    w = jnp.concatenate(cols, axis=1)
    assert w.shape[1] == EV_COLS
    return w.astype(BF16)


def _block_diag(w):
    h, a, b = w.shape
    eye = jnp.eye(h, dtype=w.dtype)
    return (eye[:, None, :, None] * w[:, :, None, :]).reshape(h * a, h * b)


def _even_proj_kernel(x_ref, g_ref, w_ref, wuk_ref, kvg_ref, t64_ref, t32_ref,
                      qc_ref, kc_ref, iq_ref, ik_ref, iw_ref, qb_ref, kb_ref, vb_ref):
    h = _rms_rows(x_ref[...], g_ref[...]).astype(BF16)

    def proj(seg):
        return _dot(h, w_ref[:, seg[0]:seg[1]])

    q_lat = _dot(proj(EV_QN).astype(BF16), wuk_ref[...]) * A_SCALE
    for hd in range(A_HEADS):
        qc_ref[hd, :, 0:LANES] = q_lat[:, LANES * hd:LANES * (hd + 1)].astype(BF16)

    cos, s_hi, s_lo = t64_ref[0], t64_ref[1], t64_ref[2]
    y = proj(EV_R64)
    half = LANES // 2
    for blk in range((EV_R64[1] - EV_R64[0]) // LANES):
        r = _rope_block(y[:, LANES * blk:LANES * (blk + 1)], cos, s_hi, s_lo, IDX_DIM // 2)
        if blk < 4:
            r = (r * IDX_DIM ** -0.5).astype(BF16)
            iq_ref[2 * blk] = r[:, :half]
            iq_ref[2 * blk + 1] = r[:, half:]
        elif blk < 8:
            r = (r * B_QK_DIM ** -0.5).astype(BF16)
            qb_ref[2 * (blk - 4)] = r[:, :half]
            qb_ref[2 * (blk - 4) + 1] = r[:, half:]
        elif blk < 12:
            r = r.astype(BF16)
            kb_ref[2 * (blk - 8)] = r[:, :half]
            kb_ref[2 * (blk - 8) + 1] = r[:, half:]
        else:
            ik_ref[...] = r[:, :half].astype(BF16)

    cos, s_hi, s_lo = t32_ref[0], t32_ref[1], t32_ref[2]
    y = proj(EV_R32)
    for blk in range(A_HEADS + 1):
        r = _rope_block(y[:, LANES * blk:LANES * (blk + 1)], cos, s_hi, s_lo, A_ROPE // 2)
        if blk < A_HEADS:
            qc_ref[blk, :, LANES:2 * LANES] = (r * A_SCALE).astype(BF16)
        else:
            kc_ref[:, LANES:2 * LANES] = r.astype(BF16)

    kc_ref[:, 0:LANES] = _rms_rows(proj(EV_CKV), kvg_ref[...]).astype(BF16)
    vb_ref[...] = proj(EV_VB).astype(BF16)
    iw_ref[...] = proj(EV_IW) * IDX_HEADS ** -0.5


def _even_proj(x2, gain, w, wuk_bd, kv_gain, t64, t32, seq):
    n = x2.shape[0]
    tm = PROJ_TM
    tiles_per_seq = seq // tm
    row = lambda i: (i, 0)
    hrow = lambda i: (0, i, 0)
    full2 = lambda i: (0, 0)
    tab = lambda i: (0, i % tiles_per_seq, 0)
    out_shape = (
        jax.ShapeDtypeStruct((A_HEADS, n, 2 * LANES), BF16),
        jax.ShapeDtypeStruct((n, 2 * LANES), BF16),
        jax.ShapeDtypeStruct((IDX_HEADS, n, IDX_DIM), BF16),
        jax.ShapeDtypeStruct((n, IDX_DIM), BF16),
        jax.ShapeDtypeStruct((n, LANES), F32),
        jax.ShapeDtypeStruct((2 * B_HEADS, n, B_QK_DIM), BF16),
        jax.ShapeDtypeStruct((2 * B_HEADS, n, B_QK_DIM), BF16),
        jax.ShapeDtypeStruct((n, B_HEADS * B_V_DIM), BF16),
    )
    return pl.pallas_call(
        _even_proj_kernel,
        grid=(n // tm,),
        in_specs=[
            pl.BlockSpec((tm, D_MODEL), row),
            pl.BlockSpec((1, D_MODEL), full2),
            pl.BlockSpec((D_MODEL, EV_COLS), full2),
            pl.BlockSpec((A_HEADS * A_NOPE, A_HEADS * A_KV_RANK), full2),
            pl.BlockSpec((1, A_KV_RANK), full2),
            pl.BlockSpec((3, tm, LANES), tab),
            pl.BlockSpec((3, tm, LANES), tab),
        ],
        out_specs=(
            pl.BlockSpec((A_HEADS, tm, 2 * LANES), hrow),
            pl.BlockSpec((tm, 2 * LANES), row),
            pl.BlockSpec((IDX_HEADS, tm, IDX_DIM), hrow),
            pl.BlockSpec((tm, IDX_DIM), row),
            pl.BlockSpec((tm, LANES), row),
            pl.BlockSpec((2 * B_HEADS, tm, B_QK_DIM), hrow),
            pl.BlockSpec((2 * B_HEADS, tm, B_QK_DIM), hrow),
            pl.BlockSpec((tm, B_HEADS * B_V_DIM), row),
        ),
        out_shape=out_shape,
        compiler_params=_cparams(("parallel",)),
        name="even_proj",
    )(x2, gain, w, wuk_bd, kv_gain, t64, t32)


def _dsa_kernel(iq_ref, ik_ref, iw_ref, qc_ref, kc_ref, o_ref, keys_ref, *, k_sel, seq):
    i = pl.program_id(1)
    qb, kc = A_QB, A_KC
    n_chunks = (i * qb + qb + kc - 1) // kc
    t_row = i * qb + lax.broadcasted_iota(jnp.int32, (qb, 1), 0)
    lane_pos = lax.broadcasted_iota(jnp.int32, (1, kc), 1)

    iq = iq_ref[...].reshape(IDX_HEADS * qb, IDX_DIM)
    iw = iw_ref[...]

    def index_chunk(c, carry):
        k = ik_ref[pl.ds(pl.multiple_of(c * kc, kc), kc), :]
        rel = jnp.maximum(_dot_nt(iq, k), 0.0).reshape(IDX_HEADS, qb, kc)
        score = rel[0] * iw[:, 0:1]
        for hd in range(1, IDX_HEADS):
            score = score + rel[hd] * iw[:, hd:hd + 1]
        bits = lax.bitcast_convert_type(score, jnp.int32)
        key = jnp.where(bits < 0, bits ^ 0x7FFFFFFF, bits)
        key = jnp.where(score == 0.0, 0, key)
        keys_ref[c] = jnp.where(c * kc + lane_pos <= t_row, key, KEY_NEG_INF)
        return carry

    lax.fori_loop(0, n_chunks, index_chunk, 0)

    def count(pred):
        def body(c, acc):
            return acc + jnp.where(pred(keys_ref[c], c), 1, 0)
        acc = lax.fori_loop(0, n_chunks, body, jnp.zeros((qb, kc), jnp.int32))
        return jnp.sum(acc, axis=1, keepdims=True)

    v = jnp.where(count(lambda key, c: key >= 0) >= k_sel, 0, INT_MIN)

    def value_bit(it, v):
        cand = v + jnp.left_shift(jnp.int32(1), 30 - it)
        return jnp.where(count(lambda key, c: key >= cand) >= k_sel, cand, v)

    v = lax.fori_loop(0, 31, value_bit, v)

    need = k_sel - count(lambda key, c: key > v)

    def index_bit(it, x):
        cand = x + jnp.left_shift(jnp.int32(1), (seq.bit_length() - 2) - it)
        below = count(lambda key, c: (key == v) & (c * kc + lane_pos < cand))
        return jnp.where(below < need, cand, x)

    j_star = lax.fori_loop(0, seq.bit_length() - 1, index_bit, jnp.zeros((qb, 1), jnp.int32))

    q = qc_ref[...].reshape(A_HEADS * qb, 2 * LANES)

    def attend(c, carry):
        m, l, acc = carry
        kv = kc_ref[pl.ds(pl.multiple_of(c * kc, kc), kc), :]
        s = _dot_nt(q, kv).reshape(A_HEADS, qb, kc)
        key = keys_ref[c]
        pos = c * kc + lane_pos
        sel = ((key > v) | ((key == v) & (pos <= j_star))) & (pos <= t_row)
        s = jnp.where(sel[None], s, MASKED)
        m_new = jnp.maximum(m, jnp.max(s, axis=-1, keepdims=True))
        alpha = jnp.exp(m - m_new)
        p = jnp.exp(s - m_new)
        l = alpha * l + jnp.sum(p, axis=-1, keepdims=True)
        pv = _dot(p.reshape(A_HEADS * qb, kc).astype(BF16), kv[:, 0:A_KV_RANK])
        acc = alpha * acc + pv.reshape(A_HEADS, qb, A_KV_RANK)
        return m_new, l, acc

    m0 = jnp.full((A_HEADS, qb, 1), MASKED, F32)
    l0 = jnp.zeros((A_HEADS, qb, 1), F32)
    a0 = jnp.zeros((A_HEADS, qb, A_KV_RANK), F32)
    _, l, acc = lax.fori_loop(0, n_chunks, attend, (m0, l0, a0))
    o = acc / l
    for hd in range(A_HEADS):
        o_ref[:, A_KV_RANK * hd:A_KV_RANK * (hd + 1)] = o[hd].astype(BF16)


def _dsa(iq, ik, iw, qc, kc, batch, seq):
    n = ik.shape[0]
    nq = seq // A_QB
    k_sel = min(TOPK_MAX, seq // 4)
    assert seq & (seq - 1) == 0 and seq % A_KC == 0 and k_sel <= A_KC
    qrow = lambda b, i: (b * nq + i, 0)
    hqrow = lambda b, i: (0, b * nq + i, 0)
    brow = lambda b, i: (b, 0)
    return pl.pallas_call(
        functools.partial(_dsa_kernel, k_sel=k_sel, seq=seq),
        grid=(batch, nq),
        in_specs=[
            pl.BlockSpec((IDX_HEADS, A_QB, IDX_DIM), hqrow),
            pl.BlockSpec((seq, IDX_DIM), brow),
            pl.BlockSpec((A_QB, LANES), qrow),
            pl.BlockSpec((A_HEADS, A_QB, 2 * LANES), hqrow),
            pl.BlockSpec((seq, 2 * LANES), brow),
        ],
        out_specs=pl.BlockSpec((A_QB, A_HEADS * A_KV_RANK), qrow),
        out_shape=jax.ShapeDtypeStruct((n, A_HEADS * A_KV_RANK), BF16),
        scratch_shapes=[pltpu.VMEM((seq // A_KC, A_QB, A_KC), jnp.int32)],
        compiler_params=_cparams(("parallel", "arbitrary")),
        name="dsa",
    )(iq, ik, iw, qc, kc)


def _diff_kernel(q_ref, k_ref, v_ref, lam_ref, g_ref, o_ref, *, lam_init):
    i = pl.program_id(1)
    qb = B_QB
    nsub = 2 * B_HEADS
    t_row = i * qb + lax.broadcasted_iota(jnp.int32, (qb, 1), 0)
    lane_pos = lax.broadcasted_iota(jnp.int32, (1, qb), 1)

    def attend(c, carry):
        m, l, acc = carry
        start = pl.multiple_of(c * qb, qb)
        s = jnp.stack([_dot_nt(q_ref[j], k_ref[j, pl.ds(start, qb), :]) for j in range(nsub)])
        causal = c * qb + lane_pos <= t_row
        s = jnp.where(causal[None], s, MASKED)
        m_new = jnp.maximum(m, jnp.max(s, axis=-1, keepdims=True))
        alpha = jnp.exp(m - m_new)
        p = jnp.exp(s - m_new)
        l = alpha * l + jnp.sum(p, axis=-1, keepdims=True)
        p = p.astype(BF16)
        pv = jnp.stack([
            _dot(p[2 * hd:2 * hd + 2].reshape(2 * qb, qb), v_ref[pl.ds(start, qb), B_V_DIM * hd:B_V_DIM * (hd + 1)])
            .reshape(2, qb, B_V_DIM) for hd in range(B_HEADS)]).reshape(nsub, qb, B_V_DIM)
        return m_new, l, alpha * acc + pv

    m0 = jnp.full((nsub, qb, 1), MASKED, F32)
    l0 = jnp.zeros((nsub, qb, 1), F32)
    a0 = jnp.zeros((nsub, qb, B_V_DIM), F32)
    _, l, acc = lax.fori_loop(0, i + 1, attend, (m0, l0, a0))
    o = acc / l

    lam = lam_ref[...]
    lam_full = (jnp.exp(jnp.sum(lam[0:1] * lam[1:2], axis=-1, keepdims=True))
                - jnp.exp(jnp.sum(lam[2:3] * lam[3:4], axis=-1, keepdims=True)) + lam_init)
    for hd in range(B_HEADS):
        d = o[2 * hd] - lam_full * o[2 * hd + 1]
        d = _rms_rows(d, g_ref[...]) * (1.0 - lam_init)
        o_ref[:, B_V_DIM * hd:B_V_DIM * (hd + 1)] = d.astype(BF16)


def _diff(qb, kb, vb, lam, subln, lam_init, batch, seq):
    n = vb.shape[0]
    nq = seq // B_QB
    return pl.pallas_call(
        functools.partial(_diff_kernel, lam_init=lam_init),
        grid=(batch, nq),
        in_specs=[
            pl.BlockSpec((2 * B_HEADS, B_QB, B_QK_DIM), lambda b, i: (0, b * nq + i, 0)),
            pl.BlockSpec((2 * B_HEADS, seq, B_QK_DIM), lambda b, i: (0, b, 0)),
            pl.BlockSpec((seq, B_HEADS * B_V_DIM), lambda b, i: (b, 0)),
            pl.BlockSpec((4, B_QK_DIM), lambda b, i: (0, 0)),
            pl.BlockSpec((1, B_V_DIM), lambda b, i: (0, 0)),
        ],
        out_specs=pl.BlockSpec((B_QB, B_HEADS * B_V_DIM), lambda b, i: (b * nq + i, 0)),
        out_shape=jax.ShapeDtypeStruct((n, B_HEADS * B_V_DIM), BF16),
        compiler_params=_cparams(("parallel", "arbitrary")),
        name="diff_attn",
    )(qb, kb, vb, lam, subln)


def _even_out_kernel(x_ref, ol_ref, ob_ref, wuv_ref, wo_ref, y_ref):
    half = A_HEADS * A_V_DIM
    o_a = _dot(ol_ref[...], wuv_ref[...]).astype(BF16)
    y_ref[...] = x_ref[...] + _dot(o_a, wo_ref[0:half, :]) + _dot(ob_ref[...], wo_ref[half:, :])


def _even_out(x2, o_lat, o_b, wuv_bd, w_out):
    n = x2.shape[0]
    tm = PROJ_TM
    row = lambda i: (i, 0)
    full2 = lambda i: (0, 0)
    return pl.pallas_call(
        _even_out_kernel,
        grid=(n // tm,),
        in_specs=[
            pl.BlockSpec((tm, D_MODEL), row),
            pl.BlockSpec((tm, A_HEADS * A_KV_RANK), row),
            pl.BlockSpec((tm, B_HEADS * B_V_DIM), row),
            pl.BlockSpec((A_HEADS * A_KV_RANK, A_HEADS * A_V_DIM), full2),
            pl.BlockSpec((D_MODEL, D_MODEL), full2),
        ],
        out_specs=pl.BlockSpec((tm, D_MODEL), row),
        out_shape=jax.ShapeDtypeStruct((n, D_MODEL), F32),
        compiler_params=_cparams(("parallel",)),
        name="even_out",
    )(x2, o_lat, o_b, wuv_bd, w_out)


OD_R64 = (0, 1792)
OD_V = (1792, 2560)
OD_G = (2560, 3072)
OD_COLS = 3072
GATES_PER_GROUP = C_HPG * 3


def _odd_weight(w_in):
    qc, kc, vc, ks, vs, kw, vw, gc = jnp.split(w_in, np.cumsum(ODD_SPLITS)[:-1].tolist(), axis=1)
    d = w_in.shape[0]
    gates = jnp.pad(gc.reshape(d, C_GROUPS, GATES_PER_GROUP), ((0, 0), (0, 0), (0, LANES - GATES_PER_GROUP)))
    w = jnp.concatenate([qc, kc, ks, kw, vc, vs, vw, gates.reshape(d, C_GROUPS * LANES)], axis=1)
    assert w.shape[1] == OD_COLS
    return w.astype(BF16)


def _odd_proj_kernel(x_ref, g_ref, w_ref, t64_ref, q_ref, kc_ref, ks_ref, kw_ref, vc_ref, vs_ref, vw_ref, gt_ref):
    h = _rms_rows(x_ref[...], g_ref[...]).astype(BF16)
    half = LANES // 2
    cos, s_hi, s_lo = t64_ref[0], t64_ref[1], t64_ref[2]
    y = _dot(h, w_ref[:, OD_R64[0]:OD_R64[1]])
    for blk in range((OD_R64[1] - OD_R64[0]) // LANES):
        r = _rope_block(y[:, LANES * blk:LANES * (blk + 1)], cos, s_hi, s_lo, C_DIM // 2)
        if blk < 8:
            r = (r * C_DIM ** -0.5).astype(BF16)
            dst, j = q_ref, blk
        else:
            r = r.astype(BF16)
            dst, j = (kc_ref, ks_ref, kw_ref)[(blk - 8) // 2], (blk - 8) % 2
        dst[2 * j] = r[:, :half]
        dst[2 * j + 1] = r[:, half:]
    y = _dot(h, w_ref[:, OD_V[0]:OD_V[1]]).astype(BF16)
    for blk in range(6):
        dst, j = (vc_ref, vs_ref, vw_ref)[blk // 2], blk % 2
        dst[2 * j] = y[:, LANES * blk:LANES * blk + half]
        dst[2 * j + 1] = y[:, LANES * blk + half:LANES * (blk + 1)]
    y = jax.nn.sigmoid(_dot(h, w_ref[:, OD_G[0]:OD_G[1]]))
    for g in range(C_GROUPS):
        gt_ref[g] = y[:, LANES * g:LANES * (g + 1)]


def _odd_proj(x2, gain, w, t64, seq):
    n = x2.shape[0]
    tm = PROJ_TM
    tiles_per_seq = seq // tm
    row = lambda i: (i, 0)
    hrow = lambda i: (0, i, 0)
    full2 = lambda i: (0, 0)
    kv_shape = jax.ShapeDtypeStruct((C_GROUPS, n, C_DIM), BF16)
    kv_spec = pl.BlockSpec((C_GROUPS, tm, C_DIM), hrow)
    return pl.pallas_call(
        _odd_proj_kernel,
        grid=(n // tm,),
        in_specs=[
            pl.BlockSpec((tm, D_MODEL), row),
            pl.BlockSpec((1, D_MODEL), full2),
            pl.BlockSpec((D_MODEL, OD_COLS), full2),
            pl.BlockSpec((3, tm, LANES), lambda i: (0, i % tiles_per_seq, 0)),
        ],
        out_specs=(pl.BlockSpec((C_HEADS, tm, C_DIM), hrow),) + (kv_spec,) * 6
        + (pl.BlockSpec((C_GROUPS, tm, LANES), hrow),),
        out_shape=(jax.ShapeDtypeStruct((C_HEADS, n, C_DIM), BF16),) + (kv_shape,) * 6
        + (jax.ShapeDtypeStruct((C_GROUPS, n, LANES), F32),),
        compiler_params=_cparams(("parallel",)),
        name="odd_proj",
    )(x2, gain, w, t64)


def _gelu_tanh(x):
    return 0.5 * x * (1.0 + jnp.tanh(math.sqrt(2.0 / math.pi) * (x + 0.044715 * (x * x * x))))


def _compress_kernel(kch_ref, vch_ref, pe_ref, w1_ref, w2_ref, kc_ref, vc_ref):
    rows = kch_ref.shape[2]
    for kv, (src, dst) in enumerate(((kch_ref, kc_ref), (vch_ref, vc_ref))):
        ch = src[0, 0].astype(F32)
        first = _dot((ch + pe_ref[kv, 0:1, :]).astype(BF16), w1_ref[kv, 0])
        second = _dot((ch + pe_ref[kv, 1:2, :]).astype(BF16), w1_ref[kv, 1])
        hid = _gelu_tanh(first + pltpu.roll(second, rows - 1, 0))
        dst[0, 0] = _dot(hid.astype(BF16), w2_ref[kv]).astype(BF16)


def _compress(kc_raw, vc_raw, pe, w1, w2, batch, seq):
    nchunk = seq // CMP_STRIDE
    width = CMP_STRIDE * C_DIM
    kch = kc_raw.reshape(C_GROUPS, batch, nchunk, width)
    vch = vc_raw.reshape(C_GROUPS, batch, nchunk, width)
    pe2 = pe.reshape(2, 2, width)
    w1s = w1.reshape(2, 2, width, CMP_HIDDEN).astype(BF16)
    w2s = w2.astype(BF16)
    blk = lambda g, b: (g, b, 0, 0)
    full3 = lambda g, b: (0, 0, 0)
    out = jax.ShapeDtypeStruct((C_GROUPS, batch, nchunk, C_DIM), BF16)
    return pl.pallas_call(
        _compress_kernel,
        grid=(C_GROUPS, batch),
        in_specs=[
            pl.BlockSpec((1, 1, nchunk, width), blk),
            pl.BlockSpec((1, 1, nchunk, width), blk),
            pl.BlockSpec((2, 2, width), full3),
            pl.BlockSpec((2, 2, width, CMP_HIDDEN), lambda g, b: (0, 0, 0, 0)),
            pl.BlockSpec((2, CMP_HIDDEN, C_DIM), full3),
        ],
        out_specs=(pl.BlockSpec((1, 1, nchunk, C_DIM), blk),) * 2,
        out_shape=(out, out),
        compiler_params=_cparams(("parallel", "parallel")),
        name="nsa_compress",
    )(kch, vch, pe2, w1s, w2s)


def _split3(x):
    a = x.astype(BF16)
    r = x - a.astype(F32)
    b = r.astype(BF16)
    c = (r - b.astype(F32)).astype(BF16)
    return a, b, c


def _nsa_kernel(q_ref, kc_ref, vc_ref, ks_ref, vs_ref, kw_ref, vw_ref, gt_ref, o_ref, *, n_blk, top_n):
    i = pl.program_id(2)
    qb = C_QB
    rows = C_HPG * qb
    q = q_ref[...].reshape(rows, C_DIM)
    t_row = i * qb + lax.broadcasted_iota(jnp.int32, (qb, 1), 0)
    lane_pos = lax.broadcasted_iota(jnp.int32, (1, qb), 1)

    n_cmp = kc_ref.shape[2]
    cmp_end = lax.broadcasted_iota(jnp.int32, (1, n_cmp), 1) * CMP_STRIDE + (CMP_LEN - 1)
    vis = cmp_end <= t_row
    s = _dot_nt(q, kc_ref[0, 0]).reshape(C_HPG, qb, n_cmp)
    s = jnp.where(vis[None], s, MASKED)
    e = jnp.where(vis[None], jnp.exp(s - jnp.max(s, axis=-1, keepdims=True)), 0.0)
    den = jnp.sum(e, axis=-1, keepdims=True)
    p_cmp = e / jnp.where(den > 0, den, 1.0)
    o_cmp = _dot(p_cmp.reshape(rows, n_cmp).astype(BF16), vc_ref[0, 0])

    p_sum = p_cmp[0]
    for hd in range(1, C_HPG):
        p_sum = p_sum + p_cmp[hd]
    blk_id = lax.broadcasted_iota(jnp.int32, (n_blk, n_cmp), 0)
    cmp_id = lax.broadcasted_iota(jnp.int32, (n_blk, n_cmp), 1)
    ratio = SEL_BLOCK // CMP_STRIDE
    overlap_t = ((cmp_id < ratio * (blk_id + 1)) & (cmp_id + CMP_LEN // CMP_STRIDE > ratio * blk_id))
    overlap_t = jnp.where(overlap_t, 1.0, 0.0).astype(BF16)
    imp = sum(_dot_nt(overlap_t, part) for part in _split3(p_sum))
    t_lane = i * qb + lane_pos
    j_sub = lax.broadcasted_iota(jnp.int32, (n_blk, 1), 0)
    cur = t_lane // SEL_BLOCK
    forced = (j_sub == 0) | (j_sub == cur) | (j_sub == cur - 1)
    imp = jnp.where(forced, jnp.inf, imp)
    imp = jnp.where(j_sub * SEL_BLOCK <= t_lane, imp, -jnp.inf)
    rank = jnp.zeros((n_blk, qb), jnp.int32)
    for other in range(n_blk):
        row = imp[other:other + 1, :]
        ahead = (row > imp) | ((row == imp) & (j_sub > other))
        rank = rank + jnp.where(ahead, 1, 0)
    sel = jnp.where(rank < top_n, 1.0, 0.0).T.astype(BF16)

    def softmax_step(carry, s, mask, v):
        m, l, acc = carry
        s = jnp.where(mask[None], s.reshape(C_HPG, qb, qb), MASKED)
        m_new = jnp.maximum(m, jnp.max(s, axis=-1, keepdims=True))
        alpha = jnp.exp(m - m_new)
        p = jnp.where(mask[None], jnp.exp(s - m_new), 0.0)
        l = alpha * l + jnp.sum(p, axis=-1, keepdims=True)
        pv = _dot(p.reshape(rows, qb).astype(BF16), v).reshape(C_HPG, qb, C_DIM)
        return m_new, l, alpha * acc + pv

    init = (jnp.full((C_HPG, qb, 1), MASKED, F32), jnp.zeros((C_HPG, qb, 1), F32), jnp.zeros((C_HPG, qb, C_DIM), F32))

    blk_of_lane = lax.broadcasted_iota(jnp.int32, (n_blk, qb), 1) // SEL_BLOCK
    blk_row = lax.broadcasted_iota(jnp.int32, (n_blk, qb), 0)

    def slc_chunk(c, carry):
        start = pl.multiple_of(c * qb, qb)
        expand = jnp.where(blk_row == blk_of_lane + c * (qb // SEL_BLOCK), 1.0, 0.0).astype(BF16)
        mask = (_dot(sel, expand) > 0.5) & (c * qb + lane_pos <= t_row)
        return softmax_step(carry, _dot_nt(q, ks_ref[0, pl.ds(start, qb), :]), mask, vs_ref[0, pl.ds(start, qb), :])

    _, l_slc, a_slc = lax.fori_loop(0, i + 1, slc_chunk, init)

    def win_chunk(c, carry):
        start = pl.multiple_of(c * qb, qb)
        dist = t_row - (c * qb + lane_pos)
        mask = (dist >= 0) & (dist < WINDOW)
        return softmax_step(carry, _dot_nt(q, kw_ref[0, pl.ds(start, qb), :]), mask, vw_ref[0, pl.ds(start, qb), :])

    _, l_win, a_win = lax.fori_loop(jnp.maximum(i - WINDOW // qb, 0), i + 1, win_chunk, init)

    o_slc = a_slc / l_slc
    o_win = a_win / l_win
    o_cmp = o_cmp.reshape(C_HPG, qb, C_DIM)
    gates = gt_ref[0]
    outs = []
    for hd in range(C_HPG):
        g0, g1, g2 = (gates[:, 3 * hd + j:3 * hd + j + 1] for j in range(3))
        outs.append(g0 * o_cmp[hd] + g1 * o_slc[hd] + g2 * o_win[hd])
    o_ref[...] = jnp.concatenate(outs, axis=-1).astype(BF16)


def _nsa(q, k_cmp, v_cmp, ks, vs, kw, vw, gates, batch, seq):
    n = ks.shape[1]
    nq = seq // C_QB
    n_blk = seq // SEL_BLOCK
    top_n = min(SEL_TOPN, n_blk)
    nchunk = seq // CMP_STRIDE
    cmp_spec = pl.BlockSpec((1, 1, nchunk, C_DIM), lambda b, g, i: (g, b, 0, 0))
    kv_spec = pl.BlockSpec((1, seq, C_DIM), lambda b, g, i: (g, b, 0))
    return pl.pallas_call(
        functools.partial(_nsa_kernel, n_blk=n_blk, top_n=top_n),
        grid=(batch, C_GROUPS, nq),
        in_specs=[
            pl.BlockSpec((C_HPG, C_QB, C_DIM), lambda b, g, i: (g, b * nq + i, 0)),
            cmp_spec, cmp_spec, kv_spec, kv_spec, kv_spec, kv_spec,
            pl.BlockSpec((1, C_QB, LANES), lambda b, g, i: (g, b * nq + i, 0)),
        ],
        out_specs=pl.BlockSpec((C_QB, C_HPG * C_DIM), lambda b, g, i: (b * nq + i, g)),
        out_shape=jax.ShapeDtypeStruct((n, C_HEADS * C_DIM), BF16),
        compiler_params=_cparams(("parallel", "parallel", "arbitrary")),
        name="nsa",
    )(q, k_cmp, v_cmp, ks, vs, kw, vw, gates)


def _odd_out_kernel(x_ref, o_ref, wo_ref, y_ref):
    y_ref[...] = x_ref[...] + _dot(o_ref[...], wo_ref[...])


def _odd_out(x2, o_c, w_out):
    n = x2.shape[0]
    tm = PROJ_TM
    row = lambda i: (i, 0)
    return pl.pallas_call(
        _odd_out_kernel,
        grid=(n // tm,),
        in_specs=[
            pl.BlockSpec((tm, D_MODEL), row),
            pl.BlockSpec((tm, D_MODEL), row),
            pl.BlockSpec((D_MODEL, D_MODEL), lambda i: (0, 0)),
        ],
        out_specs=pl.BlockSpec((tm, D_MODEL), row),
        out_shape=jax.ShapeDtypeStruct((n, D_MODEL), F32),
        compiler_params=_cparams(("parallel",)),
        name="odd_out",
    )(x2, o_c, w_out)


FFN_CHUNKS = 2


def _ffn_kernel(x_ref, g_ref, wg_ref, wu_ref, wd_ref, gf_ref, y_ref, h_ref, acc_ref, *, final_norm):
    f = pl.program_id(1)

    @pl.when(f == 0)
    def _():
        h_ref[...] = _rms_rows(x_ref[...], g_ref[...]).astype(BF16)

    h = h_ref[...]
    gate = _dot(h, wg_ref[...])
    act = (gate * jax.nn.sigmoid(gate) * _dot(h, wu_ref[...])).astype(BF16)
    part = _dot(act, wd_ref[...])

    @pl.when(f == 0)
    def _():
        acc_ref[...] = part

    @pl.when(f == FFN_CHUNKS - 1)
    def _():
        y = x_ref[...] + acc_ref[...] + part if FFN_CHUNKS > 1 else x_ref[...] + part
        y_ref[...] = _rms_rows(y, gf_ref[...]) if final_norm else y


def _ffn(x2, gain, wg, wu, wd, g_final, final_norm):
    n = x2.shape[0]
    tm = PROJ_TM
    fc = D_FF // FFN_CHUNKS
    assert FFN_CHUNKS in (1, 2) and fc % LANES == 0
    row = lambda i, f: (i, 0)
    full2 = lambda i, f: (0, 0)
    return pl.pallas_call(
        functools.partial(_ffn_kernel, final_norm=final_norm),
        grid=(n // tm, FFN_CHUNKS),
        in_specs=[
            pl.BlockSpec((tm, D_MODEL), row),
            pl.BlockSpec((1, D_MODEL), full2),
            pl.BlockSpec((D_MODEL, fc), lambda i, f: (0, f)),
            pl.BlockSpec((D_MODEL, fc), lambda i, f: (0, f)),
            pl.BlockSpec((fc, D_MODEL), lambda i, f: (f, 0)),
            pl.BlockSpec((1, D_MODEL), full2),
        ],
        out_specs=pl.BlockSpec((tm, D_MODEL), row),
        out_shape=jax.ShapeDtypeStruct((n, D_MODEL), F32),
        scratch_shapes=[pltpu.VMEM((tm, D_MODEL), BF16), pltpu.VMEM((tm, D_MODEL), F32)],
        compiler_params=_cparams(("parallel", "arbitrary")),
        name="ffn",
    )(x2, gain, wg, wu, wd, g_final)


def kernel(x, norm_mix, norm_ffn, norm_final, ev_w_in, ev_kv_gain, ev_w_uk, ev_w_uv, ev_lambda, ev_subln, ev_w_out,
           od_w_in, od_cmp_pe, od_cmp_w1, od_cmp_w2, od_w_out, ffn_w_gate, ffn_w_up, ffn_w_down):
    batch, seq, d = x.shape
    depth = norm_mix.shape[0]
    x2 = x.reshape(batch * seq, d)
    t64 = _rope_tables(seq, 64)
    t32 = _rope_tables(seq, A_ROPE)
    g_final = norm_final.reshape(1, d)
    for layer in range(depth):
        j = layer // 2
        gain = norm_mix[layer].reshape(1, d)
        if layer % 2 == 0:
            qc, kc, iq, ik, iw, qb, kb, vb = _even_proj(
                x2, gain, _even_weight(ev_w_in[j]), _block_diag(ev_w_uk[j]).astype(BF16),
                ev_kv_gain[j].reshape(1, A_KV_RANK), t64, t32, seq)
            o_lat = _dsa(iq, ik, iw, qc, kc, batch, seq)
            lam_init = 0.8 - 0.6 * math.exp(-0.3 * layer)
            o_b = _diff(qb, kb, vb, ev_lambda[j], ev_subln[j].reshape(1, B_V_DIM), lam_init, batch, seq)
            x2 = _even_out(x2, o_lat, o_b, _block_diag(ev_w_uv[j]).astype(BF16), ev_w_out[j].astype(BF16))
        else:
            q, kc_raw, ks, kw, vc_raw, vs, vw, gates = _odd_proj(x2, gain, _odd_weight(od_w_in[j]), t64, seq)
            k_cmp, v_cmp = _compress(kc_raw, vc_raw, od_cmp_pe[j], od_cmp_w1[j], od_cmp_w2[j], batch, seq)
            o_c = _nsa(q, k_cmp, v_cmp, ks, vs, kw, vw, gates, batch, seq)
            x2 = _odd_out(x2, o_c, od_w_out[j].astype(BF16))
        x2 = _ffn(x2, norm_ffn[layer].reshape(1, d), ffn_w_gate[layer].astype(BF16), ffn_w_up[layer].astype(BF16),
                  ffn_w_down[layer].astype(BF16), g_final, layer == depth - 1)
    return x2.reshape(batch, seq, d)
```

```python
import functools
import math

import numpy as np
import jax
import jax.numpy as jnp
from jax import lax
from jax.experimental import pallas as pl
from jax.experimental.pallas import tpu as pltpu

F32 = jnp.float32
BF16 = jnp.bfloat16

D_MODEL = 1024
ROPE_THETA = 10000.0
NORM_EPS = 1e-6

A_HEADS = 8
A_NOPE = 64
A_ROPE = 32
A_KV_RANK = 128
A_V_DIM = 64
A_SCALE = (A_NOPE + A_ROPE) ** -0.5
IDX_HEADS = 8
IDX_DIM = 64
TOPK_MAX = 256

B_HEADS = 4
B_QK_DIM = 64
B_V_DIM = 2 * B_QK_DIM

C_HEADS = 16
C_GROUPS = 4
C_HPG = C_HEADS // C_GROUPS
C_DIM = 64
CMP_LEN = 32
CMP_STRIDE = 16
CMP_HIDDEN = 128
SEL_BLOCK = 64
SEL_TOPN = 16
WINDOW = 512

D_FF = -(-8 * D_MODEL // (3 * 256)) * 256

EVEN_SPLITS = [A_HEADS * A_NOPE, A_HEADS * A_ROPE, A_KV_RANK, A_ROPE, IDX_HEADS * IDX_DIM, IDX_DIM, IDX_HEADS,
               B_HEADS * 2 * B_QK_DIM, B_HEADS * 2 * B_QK_DIM, B_HEADS * B_V_DIM]
ODD_SPLITS = [C_HEADS * C_DIM] + [C_GROUPS * C_DIM] * 6 + [C_HEADS * 3]

LANES = 128
SUBLANES = 8
VMEM_LIMIT = 56 * 1024 * 1024
MASKED = -1e30
KEY_NEG_INF = -2139095041
INT_MIN = -2147483648
LOG2E = math.log2(math.e)

PROJ_TM = 512
A_QB = 128
KEY_CHUNK = 256
B_QB = 256
C_QB = 256


def _cparams(sem):
    return pltpu.CompilerParams(dimension_semantics=sem, vmem_limit_bytes=VMEM_LIMIT)


def _dot(a, b):
    return jnp.dot(a, b, preferred_element_type=F32)


def _dot_nt(a, b):
    return lax.dot_general(a, b, (((1,), (1,)), ((), ())), preferred_element_type=F32)


def _rms_rows(x, gain):
    return x * lax.rsqrt(jnp.mean(x * x, axis=-1, keepdims=True) + NORM_EPS) * gain


def _rope_block(y, cos, s_hi, s_lo, half):
    return y * cos + pltpu.roll(y, half, 1) * s_hi + pltpu.roll(y, LANES - half, 1) * s_lo


def _rope_tables(seq, d):
    half = d // 2
    pos = jnp.arange(seq, dtype=jnp.int32)
    inv = ROPE_THETA ** (-jnp.arange(0, d, 2, dtype=F32) / d)
    ang = pos.astype(F32)[:, None] * inv[None, :]
    cos, sin = jnp.cos(ang), jnp.sin(ang)
    lane = np.arange(LANES)
    idx = (lane % d) % half
    first = jnp.asarray((lane % d) < half)
    c = cos[:, idx]
    s = sin[:, idx]
    return jnp.stack([c, jnp.where(first[None, :], 0.0, s), jnp.where(first[None, :], -s, 0.0)])


def _flash_tile(k, q, vt, bias, state):
    s = _dot_nt(k, q)
    if bias is not None:
        s = s + bias
    return _flash_update(s, vt, state)


def _flash_update(s, vt, state):
    m, l, acc = state
    m_new = jnp.maximum(m, jnp.max(s, axis=0, keepdims=True))
    alpha = jnp.exp2(m - m_new)
    p = jnp.exp2(s - m_new)
    l = alpha * l + jnp.sum(p, axis=0, keepdims=True)
    acc = alpha * acc + _dot(vt, p.astype(BF16))
    return m_new, l, acc


def _flash_init(dv, lanes):
    return (jnp.full((1, lanes), MASKED, F32), jnp.zeros((1, lanes), F32), jnp.zeros((dv, lanes), F32))


EV_QN = (0, 512)
EV_R64 = (512, 2176)
EV_R32 = (2176, 3328)
EV_CKV = (3328, 3456)
EV_VB = (3456, 3968)
EV_IW = (3968, 4096)
EV_COLS = 4096


def _even_weight(w_in):
    qa_nope, qa_rope, c_kv, ka_rope, iq, ik, iw, qb, kb, vb = jnp.split(w_in, np.cumsum(EVEN_SPLITS)[:-1].tolist(), axis=1)
    d = w_in.shape[0]
    z = lambda n: jnp.zeros((d, n), w_in.dtype)
    qr = jnp.pad(qa_rope.reshape(d, A_HEADS, A_ROPE), ((0, 0), (0, 0), (0, LANES - A_ROPE))).reshape(d, A_HEADS * LANES)
    cols = [qa_nope, iq, qb, kb, ik, z(LANES - IDX_DIM), qr, ka_rope, z(LANES - A_ROPE), c_kv, vb, iw, z(LANES - IDX_HEADS)]
    w = jnp.concatenate(cols, axis=1)
    assert w.shape[1] == EV_COLS
    return w.astype(BF16)


def _block_diag(w):
    h, a, b = w.shape
    eye = jnp.eye(h, dtype=w.dtype)
    return (eye[:, None, :, None] * w[:, :, None, :]).reshape(h * a, h * b)


def _even_proj_kernel(x_ref, g_ref, w_ref, wuk_ref, kvg_ref, t64_ref, t32_ref,
                      qc_ref, kc_ref, cvt_ref, iq_ref, ik_ref, iwt_ref, qb_ref, kb_ref, vbt_ref):
    h = _rms_rows(x_ref[...], g_ref[...]).astype(BF16)

    def proj(seg):
        return _dot(h, w_ref[:, seg[0]:seg[1]])

    q_lat = _dot(proj(EV_QN).astype(BF16), wuk_ref[...]) * (A_SCALE * LOG2E)
    for hd in range(A_HEADS):
        qc_ref[hd, :, 0:LANES] = q_lat[:, LANES * hd:LANES * (hd + 1)].astype(BF16)

    cos, s_hi, s_lo = t64_ref[0], t64_ref[1], t64_ref[2]
    y = proj(EV_R64)
    half = LANES // 2
    for blk in range((EV_R64[1] - EV_R64[0]) // LANES):
        r = _rope_block(y[:, LANES * blk:LANES * (blk + 1)], cos, s_hi, s_lo, IDX_DIM // 2)
        if blk < 4:
            r = (r * IDX_DIM ** -0.5).astype(BF16)
            iq_ref[2 * blk] = r[:, :half]
            iq_ref[2 * blk + 1] = r[:, half:]
        elif blk < 8:
            r = (r * (B_QK_DIM ** -0.5 * LOG2E)).astype(BF16)
            qb_ref[2 * (blk - 4)] = r[:, :half]
            qb_ref[2 * (blk - 4) + 1] = r[:, half:]
        elif blk < 12:
            r = r.astype(BF16)
            kb_ref[2 * (blk - 8)] = r[:, :half]
            kb_ref[2 * (blk - 8) + 1] = r[:, half:]
        else:
            ik_ref[...] = r[:, :half].astype(BF16)

    cos, s_hi, s_lo = t32_ref[0], t32_ref[1], t32_ref[2]
    y = proj(EV_R32)
    for blk in range(A_HEADS + 1):
        r = _rope_block(y[:, LANES * blk:LANES * (blk + 1)], cos, s_hi, s_lo, A_ROPE // 2)
        if blk < A_HEADS:
            qc_ref[blk, :, LANES:2 * LANES] = (r * (A_SCALE * LOG2E)).astype(BF16)
        else:
            kc_ref[:, LANES:2 * LANES] = r.astype(BF16)

    c_kv = _rms_rows(proj(EV_CKV), kvg_ref[...])
    kc_ref[:, 0:LANES] = c_kv.astype(BF16)
    cvt_ref[...] = c_kv.T.astype(BF16)
    vbt_ref[...] = proj(EV_VB).T.astype(BF16)
    iwt_ref[...] = (proj(EV_IW) * IDX_HEADS ** -0.5).T[0:IDX_HEADS, :]


def _even_proj(x2, gain, w, wuk_bd, kv_gain, t64, t32, seq):
    n = x2.shape[0]
    tm = PROJ_TM
    tiles_per_seq = seq // tm
    row = lambda i: (i, 0)
    col = lambda i: (0, i)
    hrow = lambda i: (0, i, 0)
    full2 = lambda i: (0, 0)
    tab = lambda i: (0, i % tiles_per_seq, 0)
    out_shape = (
        jax.ShapeDtypeStruct((A_HEADS, n, 2 * LANES), BF16),
        jax.ShapeDtypeStruct((n, 2 * LANES), BF16),
        jax.ShapeDtypeStruct((A_KV_RANK, n), BF16),
        jax.ShapeDtypeStruct((IDX_HEADS, n, IDX_DIM), BF16),
        jax.ShapeDtypeStruct((n, IDX_DIM), BF16),
        jax.ShapeDtypeStruct((IDX_HEADS, n), F32),
        jax.ShapeDtypeStruct((2 * B_HEADS, n, B_QK_DIM), BF16),
        jax.ShapeDtypeStruct((2 * B_HEADS, n, B_QK_DIM), BF16),
        jax.ShapeDtypeStruct((B_HEADS * B_V_DIM, n), BF16),
    )
    return pl.pallas_call(
        _even_proj_kernel,
        grid=(n // tm,),
        in_specs=[
            pl.BlockSpec((tm, D_MODEL), row),
            pl.BlockSpec((1, D_MODEL), full2),
            pl.BlockSpec((D_MODEL, EV_COLS), full2),
            pl.BlockSpec((A_HEADS * A_NOPE, A_HEADS * A_KV_RANK), full2),
            pl.BlockSpec((1, A_KV_RANK), full2),
            pl.BlockSpec((3, tm, LANES), tab),
            pl.BlockSpec((3, tm, LANES), tab),
        ],
        out_specs=(
            pl.BlockSpec((A_HEADS, tm, 2 * LANES), hrow),
            pl.BlockSpec((tm, 2 * LANES), row),
            pl.BlockSpec((A_KV_RANK, tm), col),
            pl.BlockSpec((IDX_HEADS, tm, IDX_DIM), hrow),
            pl.BlockSpec((tm, IDX_DIM), row),
            pl.BlockSpec((IDX_HEADS, tm), col),
            pl.BlockSpec((2 * B_HEADS, tm, B_QK_DIM), hrow),
            pl.BlockSpec((2 * B_HEADS, tm, B_QK_DIM), hrow),
            pl.BlockSpec((B_HEADS * B_V_DIM, tm), col),
        ),
        out_shape=out_shape,
        compiler_params=_cparams(("parallel",)),
        name="even_proj",
    )(x2, gain, w, wuk_bd, kv_gain, t64, t32)


def _dsa_kernel(iq_ref, ik_ref, iwt_ref, qc_ref, kc_ref, cvt_ref, o_ref, keys_ref, jstar_ref, *, k_sel, seq):
    i = pl.program_id(1)
    qb, kc = A_QB, KEY_CHUNK
    n_chunks = (i * qb + qb + kc - 1) // kc
    t_lane = i * qb + lax.broadcasted_iota(jnp.int32, (1, qb), 1)
    sub_pos = lax.broadcasted_iota(jnp.int32, (kc, 1), 0)
    iwt = iwt_ref[...]

    iq = iq_ref[...].reshape(IDX_HEADS * qb, IDX_DIM)

    def index_chunk(c, carry):
        k = ik_ref[pl.ds(pl.multiple_of(c * kc, kc), kc), :]
        rel_all = jnp.maximum(_dot_nt(k, iq), 0.0)
        score = None
        for hd in range(IDX_HEADS):
            rel = rel_all[:, qb * hd:qb * (hd + 1)] * iwt[hd:hd + 1, :]
            score = rel if score is None else score + rel
        bits = lax.bitcast_convert_type(score, jnp.int32)
        key = jnp.where(bits < 0, bits ^ 0x7FFFFFFF, bits)
        key = jnp.where(score == 0.0, 0, key)
        keys_ref[c] = jnp.where(c * kc + sub_pos <= t_lane, key, KEY_NEG_INF)
        return carry

    lax.fori_loop(0, n_chunks, index_chunk, 0)

    def count(pred):
        def body(c, acc):
            hit = jnp.where(pred(keys_ref[c], c), 1, 0)
            return acc + jnp.sum(hit.reshape(kc // SUBLANES, SUBLANES, qb), axis=0)
        acc = lax.fori_loop(0, n_chunks, body, jnp.zeros((SUBLANES, qb), jnp.int32))
        return jnp.sum(acc, axis=0, keepdims=True)

    v = jnp.where(count(lambda key, c: key >= 0) >= k_sel, 0, INT_MIN)

    def value_bit(it, v):
        cand = v + jnp.left_shift(jnp.int32(1), 30 - it)
        return jnp.where(count(lambda key, c: key >= cand) >= k_sel, cand, v)

    v = lax.fori_loop(0, 31, value_bit, v)

    jstar_ref[...] = jnp.full(jstar_ref.shape, seq - 1, jnp.int32)

    @pl.when(jnp.max(count(lambda key, c: key >= v)) > k_sel)
    def _():
        need = k_sel - count(lambda key, c: key > v)

        def index_bit(it, x):
            cand = x + jnp.left_shift(jnp.int32(1), (seq.bit_length() - 2) - it)
            below = count(lambda key, c: (key == v) & (c * kc + sub_pos < cand))
            return jnp.where(below < need, cand, x)

        x = lax.fori_loop(0, seq.bit_length() - 1, index_bit, jnp.zeros((1, qb), jnp.int32))
        jstar_ref[...] = jnp.broadcast_to(x, jstar_ref.shape)

    j_star = jstar_ref[0:1, :]

    q = qc_ref[...].reshape(A_HEADS * qb, 2 * LANES)

    def attend(c, state):
        start = pl.multiple_of(c * kc, kc)
        kv = kc_ref[pl.ds(start, kc), :]
        vt = cvt_ref[:, pl.ds(start, kc)]
        key = keys_ref[c]
        pos = c * kc + sub_pos
        sel = ((key > v) | ((key == v) & (pos <= j_star))) & (pos <= t_lane)
        bias = jnp.where(sel, 0.0, MASKED)
        return _flash_tile(kv, q, vt, jnp.concatenate([bias] * A_HEADS, axis=1), state)

    _, l, acc = lax.fori_loop(0, n_chunks, attend, _flash_init(A_KV_RANK, A_HEADS * qb))
    o = acc / l
    for hd in range(A_HEADS):
        o_ref[:, A_KV_RANK * hd:A_KV_RANK * (hd + 1)] = o[:, qb * hd:qb * (hd + 1)].T.astype(BF16)


def _dsa(iq, ik, iwt, qc, kc, cvt, batch, seq):
    n = ik.shape[0]
    nq = seq // A_QB
    k_sel = min(TOPK_MAX, seq // 4)
    assert seq & (seq - 1) == 0 and seq % KEY_CHUNK == 0 and k_sel <= KEY_CHUNK and A_QB == LANES
    qrow = lambda b, i: (b * nq + i, 0)
    hqrow = lambda b, i: (0, b * nq + i, 0)
    brow = lambda b, i: (b, 0)
    return pl.pallas_call(
        functools.partial(_dsa_kernel, k_sel=k_sel, seq=seq),
        grid=(batch, nq),
        in_specs=[
            pl.BlockSpec((IDX_HEADS, A_QB, IDX_DIM), hqrow),
            pl.BlockSpec((seq, IDX_DIM), brow),
            pl.BlockSpec((IDX_HEADS, A_QB), lambda b, i: (0, b * nq + i)),
            pl.BlockSpec((A_HEADS, A_QB, 2 * LANES), hqrow),
            pl.BlockSpec((seq, 2 * LANES), brow),
            pl.BlockSpec((A_KV_RANK, seq), lambda b, i: (0, b)),
        ],
        out_specs=pl.BlockSpec((A_QB, A_HEADS * A_KV_RANK), qrow),
        out_shape=jax.ShapeDtypeStruct((n, A_HEADS * A_KV_RANK), BF16),
        scratch_shapes=[pltpu.VMEM((seq // KEY_CHUNK, KEY_CHUNK, A_QB), jnp.int32),
                        pltpu.VMEM((SUBLANES, A_QB), jnp.int32)],
        compiler_params=_cparams(("parallel", "arbitrary")),
        name="dsa",
    )(iq, ik, iwt, qc, kc, cvt)


def _diff_kernel(q_ref, k_ref, vt_ref, lam_ref, g_ref, o_ref, *, lam_init):
    i = pl.program_id(1)
    qb, kc = B_QB, KEY_CHUNK
    causal = (lax.broadcasted_iota(jnp.int32, (kc, qb), 0) <= lax.broadcasted_iota(jnp.int32, (kc, qb), 1))
    tri = jnp.where(causal, 0.0, MASKED)

    def sweep(c, states, diagonal):
        start = pl.multiple_of(c * kc, kc)
        out = []
        for hd in range(B_HEADS):
            s = jnp.concatenate([_dot_nt(k_ref[2 * hd + mp, pl.ds(start, kc), :], q_ref[2 * hd + mp]) for mp in range(2)],
                                axis=1)
            if diagonal:
                s = s + jnp.concatenate([tri, tri], axis=1)
            vt = vt_ref[B_V_DIM * hd:B_V_DIM * (hd + 1), pl.ds(start, kc)]
            out.append(_flash_update(s, vt, states[hd]))
        return tuple(out)

    states = tuple(_flash_init(B_V_DIM, 2 * qb) for _ in range(B_HEADS))
    states = lax.fori_loop(0, i, lambda c, s: sweep(c, s, False), states)
    states = sweep(i, states, True)

    lam = lam_ref[...]
    lam_full = (jnp.exp(jnp.sum(lam[0:1] * lam[1:2], axis=-1, keepdims=True))
                - jnp.exp(jnp.sum(lam[2:3] * lam[3:4], axis=-1, keepdims=True)) + lam_init)
    gain_col = g_ref[...]
    for hd in range(B_HEADS):
        _, l, acc = states[hd]
        o = acc / l
        d = o[:, 0:qb] - lam_full * o[:, qb:2 * qb]
        d = d * lax.rsqrt(jnp.mean(d * d, axis=0, keepdims=True) + NORM_EPS) * gain_col * (1.0 - lam_init)
        o_ref[:, B_V_DIM * hd:B_V_DIM * (hd + 1)] = d.T.astype(BF16)


def _diff(qb, kb, vbt, lam, subln, lam_init, batch, seq):
    n = vbt.shape[1]
    nq = seq // B_QB
    assert B_QB == KEY_CHUNK
    return pl.pallas_call(
        functools.partial(_diff_kernel, lam_init=lam_init),
        grid=(batch, nq),
        in_specs=[
            pl.BlockSpec((2 * B_HEADS, B_QB, B_QK_DIM), lambda b, i: (0, b * nq + i, 0)),
            pl.BlockSpec((2 * B_HEADS, seq, B_QK_DIM), lambda b, i: (0, b, 0)),
            pl.BlockSpec((B_HEADS * B_V_DIM, seq), lambda b, i: (0, b)),
            pl.BlockSpec((4, B_QK_DIM), lambda b, i: (0, 0)),
            pl.BlockSpec((B_V_DIM, 1), lambda b, i: (0, 0)),
        ],
        out_specs=pl.BlockSpec((B_QB, B_HEADS * B_V_DIM), lambda b, i: (b * nq + i, 0)),
        out_shape=jax.ShapeDtypeStruct((n, B_HEADS * B_V_DIM), BF16),
        compiler_params=_cparams(("parallel", "arbitrary")),
        name="diff_attn",
    )(qb, kb, vbt, lam, subln)


def _even_out_kernel(x_ref, ol_ref, ob_ref, wuv_ref, wo_ref, y_ref):
    half = A_HEADS * A_V_DIM
    o_a = _dot(ol_ref[...], wuv_ref[...]).astype(BF16)
    y_ref[...] = x_ref[...] + _dot(o_a, wo_ref[0:half, :]) + _dot(ob_ref[...], wo_ref[half:, :])


def _even_out(x2, o_lat, o_b, wuv_bd, w_out):
    n = x2.shape[0]
    tm = PROJ_TM
    row = lambda i: (i, 0)
    full2 = lambda i: (0, 0)
    return pl.pallas_call(
        _even_out_kernel,
        grid=(n // tm,),
        in_specs=[
            pl.BlockSpec((tm, D_MODEL), row),
            pl.BlockSpec((tm, A_HEADS * A_KV_RANK), row),
            pl.BlockSpec((tm, B_HEADS * B_V_DIM), row),
            pl.BlockSpec((A_HEADS * A_KV_RANK, A_HEADS * A_V_DIM), full2),
            pl.BlockSpec((D_MODEL, D_MODEL), full2),
        ],
        out_specs=pl.BlockSpec((tm, D_MODEL), row),
        out_shape=jax.ShapeDtypeStruct((n, D_MODEL), F32),
        compiler_params=_cparams(("parallel",)),
        name="even_out",
    )(x2, o_lat, o_b, wuv_bd, w_out)


OD_R64 = (0, 1792)
OD_V = (1792, 2560)
OD_G = (2560, 3072)
OD_COLS = 3072
GATES_PER_GROUP = C_HPG * 3
KV_WIDTH = C_GROUPS * C_DIM


def _odd_weight(w_in):
    qc, kc, vc, ks, vs, kw, vw, gc = jnp.split(w_in, np.cumsum(ODD_SPLITS)[:-1].tolist(), axis=1)
    d = w_in.shape[0]
    gates = jnp.pad(gc.reshape(d, C_GROUPS, GATES_PER_GROUP), ((0, 0), (0, 0), (0, LANES - GATES_PER_GROUP)))
    w = jnp.concatenate([qc, kc, ks, kw, vc, vs, vw, gates.reshape(d, C_GROUPS * LANES)], axis=1)
    assert w.shape[1] == OD_COLS
    return w.astype(BF16)


def _odd_proj_kernel(x_ref, g_ref, w_ref, t64_ref, q_ref, kc_ref, ks_ref, kw_ref, vc_ref, vst_ref, vwt_ref, gt_ref):
    h = _rms_rows(x_ref[...], g_ref[...]).astype(BF16)
    half = LANES // 2
    cos, s_hi, s_lo = t64_ref[0], t64_ref[1], t64_ref[2]
    y = _dot(h, w_ref[:, OD_R64[0]:OD_R64[1]])
    for blk in range((OD_R64[1] - OD_R64[0]) // LANES):
        r = _rope_block(y[:, LANES * blk:LANES * (blk + 1)], cos, s_hi, s_lo, C_DIM // 2)
        if blk < 8:
            r = (r * (C_DIM ** -0.5 * LOG2E)).astype(BF16)
            dst, j = q_ref, blk
        else:
            r = r.astype(BF16)
            dst, j = (kc_ref, ks_ref, kw_ref)[(blk - 8) // 2], (blk - 8) % 2
        dst[2 * j] = r[:, :half]
        dst[2 * j + 1] = r[:, half:]
    y = _dot(h, w_ref[:, OD_V[0]:OD_V[1]])
    vc = y[:, 0:KV_WIDTH].astype(BF16)
    for g in range(C_GROUPS):
        vc_ref[g] = vc[:, C_DIM * g:C_DIM * (g + 1)]
    vst_ref[...] = y[:, KV_WIDTH:2 * KV_WIDTH].T.astype(BF16)
    vwt_ref[...] = y[:, 2 * KV_WIDTH:3 * KV_WIDTH].T.astype(BF16)
    y = jax.nn.sigmoid(_dot(h, w_ref[:, OD_G[0]:OD_G[1]]))
    for g in range(C_GROUPS):
        gt_ref[g] = y[:, LANES * g:LANES * (g + 1)]


def _odd_proj(x2, gain, w, t64, seq):
    n = x2.shape[0]
    tm = PROJ_TM
    tiles_per_seq = seq // tm
    row = lambda i: (i, 0)
    col = lambda i: (0, i)
    hrow = lambda i: (0, i, 0)
    full2 = lambda i: (0, 0)
    kv_shape = jax.ShapeDtypeStruct((C_GROUPS, n, C_DIM), BF16)
    kv_spec = pl.BlockSpec((C_GROUPS, tm, C_DIM), hrow)
    kvt_shape = jax.ShapeDtypeStruct((KV_WIDTH, n), BF16)
    kvt_spec = pl.BlockSpec((KV_WIDTH, tm), col)
    return pl.pallas_call(
        _odd_proj_kernel,
        grid=(n // tm,),
        in_specs=[
            pl.BlockSpec((tm, D_MODEL), row),
            pl.BlockSpec((1, D_MODEL), full2),
            pl.BlockSpec((D_MODEL, OD_COLS), full2),
            pl.BlockSpec((3, tm, LANES), lambda i: (0, i % tiles_per_seq, 0)),
        ],
        out_specs=(pl.BlockSpec((C_HEADS, tm, C_DIM), hrow),) + (kv_spec,) * 4 + (kvt_spec,) * 2
        + (pl.BlockSpec((C_GROUPS, tm, LANES), hrow),),
        out_shape=(jax.ShapeDtypeStruct((C_HEADS, n, C_DIM), BF16),) + (kv_shape,) * 4 + (kvt_shape,) * 2
        + (jax.ShapeDtypeStruct((C_GROUPS, n, LANES), F32),),
        compiler_params=_cparams(("parallel",)),
        name="odd_proj",
    )(x2, gain, w, t64)


def _gelu_tanh(x):
    return 0.5 * x * (1.0 + jnp.tanh(math.sqrt(2.0 / math.pi) * (x + 0.044715 * (x * x * x))))


def _compress_kernel(kch_ref, vch_ref, pe_ref, w1_ref, w2_ref, w2t_ref, kc_ref, vct_ref):
    rows = kch_ref.shape[2]

    def hidden(src, kv):
        ch = src[0, 0].astype(F32)
        first = _dot((ch + pe_ref[kv, 0:1, :]).astype(BF16), w1_ref[kv, 0])
        second = _dot((ch + pe_ref[kv, 1:2, :]).astype(BF16), w1_ref[kv, 1])
        return _gelu_tanh(first + pltpu.roll(second, rows - 1, 0)).astype(BF16)

    kc_ref[0, 0] = _dot(hidden(kch_ref, 0), w2_ref[...]).astype(BF16)
    vct_ref[0, 0] = _dot_nt(w2t_ref[...], hidden(vch_ref, 1)).astype(BF16)


def _compress(kc_raw, vc_raw, pe, w1, w2, batch, seq):
    nchunk = seq // CMP_STRIDE
    width = CMP_STRIDE * C_DIM
    kch = kc_raw.reshape(C_GROUPS, batch, nchunk, width)
    vch = vc_raw.reshape(C_GROUPS, batch, nchunk, width)
    pe2 = pe.reshape(2, 2, width)
    w1s = w1.reshape(2, 2, width, CMP_HIDDEN).astype(BF16)
    blk = lambda g, b: (g, b, 0, 0)
    full2 = lambda g, b: (0, 0)
    return pl.pallas_call(
        _compress_kernel,
        grid=(C_GROUPS, batch),
        in_specs=[
            pl.BlockSpec((1, 1, nchunk, width), blk),
            pl.BlockSpec((1, 1, nchunk, width), blk),
            pl.BlockSpec((2, 2, width), lambda g, b: (0, 0, 0)),
            pl.BlockSpec((2, 2, width, CMP_HIDDEN), lambda g, b: (0, 0, 0, 0)),
            pl.BlockSpec((CMP_HIDDEN, C_DIM), full2),
            pl.BlockSpec((C_DIM, CMP_HIDDEN), full2),
        ],
        out_specs=(pl.BlockSpec((1, 1, nchunk, C_DIM), blk), pl.BlockSpec((1, 1, C_DIM, nchunk), blk)),
        out_shape=(jax.ShapeDtypeStruct((C_GROUPS, batch, nchunk, C_DIM), BF16),
                   jax.ShapeDtypeStruct((C_GROUPS, batch, C_DIM, nchunk), BF16)),
        compiler_params=_cparams(("parallel", "parallel")),
        name="nsa_compress",
    )(kch, vch, pe2, w1s, w2[0].astype(BF16), w2[1].T.astype(BF16))


def _split3(x):
    a = x.astype(BF16)
    r = x - a.astype(F32)
    b = r.astype(BF16)
    c = (r - b.astype(F32)).astype(BF16)
    return a, b, c


def _nsa_kernel(q_ref, kc_ref, vct_ref, ks_ref, vst_ref, kw_ref, vwt_ref, gt_ref, o_ref, *, n_blk, top_n):
    i = pl.program_id(2)
    qb, kc = C_QB, KEY_CHUNK
    q = q_ref[...].reshape(C_HPG * qb, C_DIM)
    heads = lambda a: jnp.concatenate([a] * C_HPG, axis=1)
    t_lane = i * qb + lax.broadcasted_iota(jnp.int32, (1, qb), 1)
    sub_pos = lax.broadcasted_iota(jnp.int32, (kc, 1), 0)

    n_cmp = kc_ref.shape[2]
    cmp_end = lax.broadcasted_iota(jnp.int32, (n_cmp, 1), 0) * CMP_STRIDE + (CMP_LEN - 1)
    vis = heads(jnp.where(cmp_end <= t_lane, 1.0, 0.0))
    s = _dot_nt(kc_ref[0, 0], q) + (vis - 1.0) * (-MASKED)
    e = jnp.exp2(s - jnp.max(s, axis=0, keepdims=True)) * vis
    den = jnp.sum(e, axis=0, keepdims=True)
    p_cmp = e / jnp.where(den > 0, den, 1.0)
    o_cmp = _dot(vct_ref[0, 0], p_cmp.astype(BF16))

    p_sum = p_cmp[:, 0:qb]
    for hd in range(1, C_HPG):
        p_sum = p_sum + p_cmp[:, qb * hd:qb * (hd + 1)]
    blk_id = lax.broadcasted_iota(jnp.int32, (n_blk, n_cmp), 0)
    cmp_id = lax.broadcasted_iota(jnp.int32, (n_blk, n_cmp), 1)
    ratio = SEL_BLOCK // CMP_STRIDE
    overlap_t = ((cmp_id < ratio * (blk_id + 1)) & (cmp_id + CMP_LEN // CMP_STRIDE > ratio * blk_id))
    overlap_t = jnp.where(overlap_t, 1.0, 0.0).astype(BF16)
    imp = sum(_dot(overlap_t, part) for part in _split3(p_sum))
    j_sub = lax.broadcasted_iota(jnp.int32, (n_blk, 1), 0)
    cur = t_lane // SEL_BLOCK
    forced = (j_sub == 0) | (j_sub == cur) | (j_sub == cur - 1)
    imp = jnp.where(forced, jnp.inf, imp)
    imp = jnp.where(j_sub * SEL_BLOCK <= t_lane, imp, -jnp.inf)
    rank = jnp.zeros((n_blk, qb), jnp.int32)
    for other in range(n_blk):
        row = imp[other:other + 1, :]
        ahead = (row > imp) | ((row == imp) & (j_sub > other))
        rank = rank + jnp.where(ahead, 1, 0)
    sel = jnp.where(rank < top_n, 1.0, 0.0).astype(BF16)

    def sweep(k_ref, vt_ref, c, bias, state):
        start = pl.multiple_of(c * kc, kc)
        return _flash_tile(k_ref[0, pl.ds(start, kc), :], q, vt_ref[:, pl.ds(start, kc)], heads(bias), state)

    init = _flash_init(C_DIM, C_HPG * qb)

    key_blk = lax.broadcasted_iota(jnp.int32, (kc, n_blk), 0) // SEL_BLOCK
    blk_col = lax.broadcasted_iota(jnp.int32, (kc, n_blk), 1)

    def slc_chunk(c, state):
        expand = jnp.where(blk_col == key_blk + c * (kc // SEL_BLOCK), 1.0, 0.0).astype(BF16)
        allowed = (_dot(expand, sel) > 0.5) & (c * kc + sub_pos <= t_lane)
        return sweep(ks_ref, vst_ref, c, jnp.where(allowed, 0.0, MASKED), state)

    _, l_slc, a_slc = lax.fori_loop(0, i + 1, slc_chunk, init)

    assert qb == kc and WINDOW % kc == 0
    win = init
    back = WINDOW // kc
    key_i = lax.broadcasted_iota(jnp.int32, (kc, qb), 0)
    lane_i = lax.broadcasted_iota(jnp.int32, (kc, qb), 1)
    for step in range(back + 1):
        c = i - back + step
        exists = jnp.where(c >= 0, 0.0, MASKED)
        if step == 0:
            bias = jnp.where(key_i > lane_i, exists, MASKED)
        elif step < back:
            bias = jnp.where(key_i >= 0, exists, MASKED)
        else:
            bias = jnp.where(key_i <= lane_i, 0.0, MASKED)
        win = sweep(kw_ref, vwt_ref, jnp.maximum(c, 0), bias, win)
    _, l_win, a_win = win

    gates_t = gt_ref[0].T
    o_slc = a_slc / l_slc
    o_win = a_win / l_win
    cols = []
    for hd in range(C_HPG):
        g0, g1, g2 = (gates_t[3 * hd + j:3 * hd + j + 1, :] for j in range(3))
        blk = slice(qb * hd, qb * (hd + 1))
        cols.append(g0 * o_cmp[:, blk] + g1 * o_slc[:, blk] + g2 * o_win[:, blk])
    o_ref[...] = jnp.concatenate(cols, axis=0).T.astype(BF16)


def _nsa(q, k_cmp, v_cmp_t, ks, vst, kw, vwt, gates, batch, seq):
    n = ks.shape[1]
    nq = seq // C_QB
    n_blk = seq // SEL_BLOCK
    top_n = min(SEL_TOPN, n_blk)
    nchunk = seq // CMP_STRIDE
    k_spec = pl.BlockSpec((1, seq, C_DIM), lambda b, g, i: (g, b, 0))
    vt_spec = pl.BlockSpec((C_DIM, seq), lambda b, g, i: (g, b))
    return pl.pallas_call(
        functools.partial(_nsa_kernel, n_blk=n_blk, top_n=top_n),
        grid=(batch, C_GROUPS, nq),
        in_specs=[
            pl.BlockSpec((C_HPG, C_QB, C_DIM), lambda b, g, i: (g, b * nq + i, 0)),
            pl.BlockSpec((1, 1, nchunk, C_DIM), lambda b, g, i: (g, b, 0, 0)),
            pl.BlockSpec((1, 1, C_DIM, nchunk), lambda b, g, i: (g, b, 0, 0)),
            k_spec, vt_spec, k_spec, vt_spec,
            pl.BlockSpec((1, C_QB, LANES), lambda b, g, i: (g, b * nq + i, 0)),
        ],
        out_specs=pl.BlockSpec((C_QB, C_HPG * C_DIM), lambda b, g, i: (b * nq + i, g)),
        out_shape=jax.ShapeDtypeStruct((n, C_HEADS * C_DIM), BF16),
        compiler_params=_cparams(("parallel", "parallel", "arbitrary")),
        name="nsa",
    )(q, k_cmp, v_cmp_t, ks, vst, kw, vwt, gates)


def _odd_out_kernel(x_ref, o_ref, wo_ref, y_ref):
    y_ref[...] = x_ref[...] + _dot(o_ref[...], wo_ref[...])


def _odd_out(x2, o_c, w_out):
    n = x2.shape[0]
    tm = PROJ_TM
    row = lambda i: (i, 0)
    return pl.pallas_call(
        _odd_out_kernel,
        grid=(n // tm,),
        in_specs=[
            pl.BlockSpec((tm, D_MODEL), row),
            pl.BlockSpec((tm, D_MODEL), row),
            pl.BlockSpec((D_MODEL, D_MODEL), lambda i: (0, 0)),
        ],
        out_specs=pl.BlockSpec((tm, D_MODEL), row),
        out_shape=jax.ShapeDtypeStruct((n, D_MODEL), F32),
        compiler_params=_cparams(("parallel",)),
        name="odd_out",
    )(x2, o_c, w_out)


FFN_CHUNKS = 2


def _ffn_kernel(x_ref, g_ref, wg_ref, wu_ref, wd_ref, gf_ref, y_ref, h_ref, acc_ref, *, final_norm):
    f = pl.program_id(1)

    @pl.when(f == 0)
    def _():
        h_ref[...] = _rms_rows(x_ref[...], g_ref[...]).astype(BF16)

    h = h_ref[...]
    gate = _dot(h, wg_ref[...])
    act = (gate * jax.nn.sigmoid(gate) * _dot(h, wu_ref[...])).astype(BF16)
    part = _dot(act, wd_ref[...])

    @pl.when(f == 0)
    def _():
        acc_ref[...] = part

    @pl.when(f == FFN_CHUNKS - 1)
    def _():
        y = x_ref[...] + acc_ref[...] + part if FFN_CHUNKS > 1 else x_ref[...] + part
        y_ref[...] = _rms_rows(y, gf_ref[...]) if final_norm else y


def _ffn(x2, gain, wg, wu, wd, g_final, final_norm):
    n = x2.shape[0]
    tm = PROJ_TM
    fc = D_FF // FFN_CHUNKS
    assert FFN_CHUNKS in (1, 2) and fc % LANES == 0
    row = lambda i, f: (i, 0)
    full2 = lambda i, f: (0, 0)
    return pl.pallas_call(
        functools.partial(_ffn_kernel, final_norm=final_norm),
        grid=(n // tm, FFN_CHUNKS),
        in_specs=[
            pl.BlockSpec((tm, D_MODEL), row),
            pl.BlockSpec((1, D_MODEL), full2),
            pl.BlockSpec((D_MODEL, fc), lambda i, f: (0, f)),
            pl.BlockSpec((D_MODEL, fc), lambda i, f: (0, f)),
            pl.BlockSpec((fc, D_MODEL), lambda i, f: (f, 0)),
            pl.BlockSpec((1, D_MODEL), full2),
        ],
        out_specs=pl.BlockSpec((tm, D_MODEL), row),
        out_shape=jax.ShapeDtypeStruct((n, D_MODEL), F32),
        scratch_shapes=[pltpu.VMEM((tm, D_MODEL), BF16), pltpu.VMEM((tm, D_MODEL), F32)],
        compiler_params=_cparams(("parallel", "arbitrary")),
        name="ffn",
    )(x2, gain, wg, wu, wd, g_final)


def kernel(x, norm_mix, norm_ffn, norm_final, ev_w_in, ev_kv_gain, ev_w_uk, ev_w_uv, ev_lambda, ev_subln, ev_w_out,
           od_w_in, od_cmp_pe, od_cmp_w1, od_cmp_w2, od_w_out, ffn_w_gate, ffn_w_up, ffn_w_down):
    batch, seq, d = x.shape
    depth = norm_mix.shape[0]
    x2 = x.reshape(batch * seq, d)
    t64 = _rope_tables(seq, 64)
    t32 = _rope_tables(seq, A_ROPE)
    g_final = norm_final.reshape(1, d)
    for layer in range(depth):
        j = layer // 2
        gain = norm_mix[layer].reshape(1, d)
        if layer % 2 == 0:
            qc, kc, cvt, iq, ik, iwt, qb, kb, vbt = _even_proj(
                x2, gain, _even_weight(ev_w_in[j]), _block_diag(ev_w_uk[j]).astype(BF16),
                ev_kv_gain[j].reshape(1, A_KV_RANK), t64, t32, seq)
            o_lat = _dsa(iq, ik, iwt, qc, kc, cvt, batch, seq)
            lam_init = 0.8 - 0.6 * math.exp(-0.3 * layer)
            o_b = _diff(qb, kb, vbt, ev_lambda[j], ev_subln[j].reshape(B_V_DIM, 1), lam_init, batch, seq)
            x2 = _even_out(x2, o_lat, o_b, _block_diag(ev_w_uv[j]).astype(BF16), ev_w_out[j].astype(BF16))
        else:
            q, kc_raw, ks, kw, vc_raw, vst, vwt, gates = _odd_proj(x2, gain, _odd_weight(od_w_in[j]), t64, seq)
            k_cmp, v_cmp_t = _compress(kc_raw, vc_raw, od_cmp_pe[j], od_cmp_w1[j], od_cmp_w2[j], batch, seq)
            o_c = _nsa(q, k_cmp, v_cmp_t, ks, vst, kw, vwt, gates, batch, seq)
            x2 = _odd_out(x2, o_c, od_w_out[j].astype(BF16))
        x2 = _ffn(x2, norm_ffn[layer].reshape(1, d), ffn_w_gate[layer].astype(BF16), ffn_w_up[layer].astype(BF16),
                  ffn_w_down[layer].astype(BF16), g_final, layer == depth - 1)
    return x2.reshape(batch, seq, d)
```

```python
import functools
import math

import numpy as np
import jax
import jax.numpy as jnp
from jax import lax
from jax.experimental import pallas as pl
from jax.experimental.pallas import tpu as pltpu

F32 = jnp.float32
BF16 = jnp.bfloat16

D_MODEL = 1024
ROPE_THETA = 10000.0
NORM_EPS = 1e-6

A_HEADS = 8
A_NOPE = 64
A_ROPE = 32
A_KV_RANK = 128
A_V_DIM = 64
A_SCALE = (A_NOPE + A_ROPE) ** -0.5
IDX_HEADS = 8
IDX_DIM = 64
TOPK_MAX = 256

B_HEADS = 4
B_QK_DIM = 64
B_V_DIM = 2 * B_QK_DIM

C_HEADS = 16
C_GROUPS = 4
C_HPG = C_HEADS // C_GROUPS
C_DIM = 64
CMP_LEN = 32
CMP_STRIDE = 16
CMP_HIDDEN = 128
SEL_BLOCK = 64
SEL_TOPN = 16
WINDOW = 512

D_FF = -(-8 * D_MODEL // (3 * 256)) * 256

EVEN_SPLITS = [A_HEADS * A_NOPE, A_HEADS * A_ROPE, A_KV_RANK, A_ROPE, IDX_HEADS * IDX_DIM, IDX_DIM, IDX_HEADS,
               B_HEADS * 2 * B_QK_DIM, B_HEADS * 2 * B_QK_DIM, B_HEADS * B_V_DIM]
ODD_SPLITS = [C_HEADS * C_DIM] + [C_GROUPS * C_DIM] * 6 + [C_HEADS * 3]

LANES = 128
SUBLANES = 8
VMEM_LIMIT = 56 * 1024 * 1024
MASKED = -1e30
KEY_NEG_INF = -2139095041
HALF16 = 1 << 15
LOG2E = math.log2(math.e)

PROJ_TM = 512
A_QB = 128
KEY_CHUNK = 256
B_QB = 256
C_QB = 256


def _cparams(sem):
    return pltpu.CompilerParams(dimension_semantics=sem, vmem_limit_bytes=VMEM_LIMIT)


def _dot(a, b):
    return jnp.dot(a, b, preferred_element_type=F32)


def _dot_nt(a, b):
    return lax.dot_general(a, b, (((1,), (1,)), ((), ())), preferred_element_type=F32)


def _rms_rows(x, gain):
    return x * lax.rsqrt(jnp.mean(x * x, axis=-1, keepdims=True) + NORM_EPS) * gain


def _rope_block(y, cos, s_hi, s_lo, half):
    return y * cos + pltpu.roll(y, half, 1) * s_hi + pltpu.roll(y, LANES - half, 1) * s_lo


def _rope_tables(seq, d):
    half = d // 2
    pos = jnp.arange(seq, dtype=jnp.int32)
    inv = ROPE_THETA ** (-jnp.arange(0, d, 2, dtype=F32) / d)
    ang = pos.astype(F32)[:, None] * inv[None, :]
    cos, sin = jnp.cos(ang), jnp.sin(ang)
    lane = np.arange(LANES)
    idx = (lane % d) % half
    first = jnp.asarray((lane % d) < half)
    c = cos[:, idx]
    s = sin[:, idx]
    return jnp.stack([c, jnp.where(first[None, :], 0.0, s), jnp.where(first[None, :], -s, 0.0)])


ONES_ROWS = 16


def _flash_update(s, vt, state):
    m, acc = state
    m_new = jnp.maximum(m, jnp.max(s, axis=0, keepdims=True))
    alpha = jnp.exp2(m - m_new)
    p = jnp.exp2(s - m_new).astype(BF16)
    vt_ones = jnp.concatenate([vt, jnp.ones((ONES_ROWS, vt.shape[1]), BF16)], axis=0)
    return m_new, alpha * acc + _dot(vt_ones, p)


def _flash_init(dv, lanes):
    return (jnp.full((1, lanes), MASKED, F32), jnp.zeros((dv + ONES_ROWS, lanes), F32))


def _flash_result(state, dv):
    _, acc = state
    return acc[0:dv] / acc[dv:dv + 1]


EV_QN = (0, 512)
EV_R64 = (512, 2176)
EV_R32 = (2176, 3328)
EV_CKV = (3328, 3456)
EV_VB = (3456, 3968)
EV_IW = (3968, 4096)
EV_COLS = 4096


def _even_weight(w_in):
    qa_nope, qa_rope, c_kv, ka_rope, iq, ik, iw, qb, kb, vb = jnp.split(w_in, np.cumsum(EVEN_SPLITS)[:-1].tolist(), axis=1)
    d = w_in.shape[0]
    z = lambda n: jnp.zeros((d, n), w_in.dtype)
    qr = jnp.pad(qa_rope.reshape(d, A_HEADS, A_ROPE), ((0, 0), (0, 0), (0, LANES - A_ROPE))).reshape(d, A_HEADS * LANES)
    cols = [qa_nope, iq, qb, kb, ik, z(LANES - IDX_DIM), qr, ka_rope, z(LANES - A_ROPE), c_kv, vb, iw, z(LANES - IDX_HEADS)]
    w = jnp.concatenate(cols, axis=1)
    assert w.shape[1] == EV_COLS
    return w.astype(BF16)


def _block_diag(w):
    h, a, b = w.shape
    eye = jnp.eye(h, dtype=w.dtype)
    return (eye[:, None, :, None] * w[:, :, None, :]).reshape(h * a, h * b)


def _even_proj_kernel(x_ref, g_ref, w_ref, wuk_ref, kvg_ref, t64_ref, t32_ref,
                      qc_ref, kc_ref, cvt_ref, iq_ref, ik_ref, iwt_ref, qb_ref, kb_ref, vbt_ref):
    h = _rms_rows(x_ref[...], g_ref[...]).astype(BF16)

    def proj(seg):
        return _dot(h, w_ref[:, seg[0]:seg[1]])

    q_lat = _dot(proj(EV_QN).astype(BF16), wuk_ref[...]) * (A_SCALE * LOG2E)
    for hd in range(A_HEADS):
        qc_ref[hd, :, 0:LANES] = q_lat[:, LANES * hd:LANES * (hd + 1)].astype(BF16)

    cos, s_hi, s_lo = t64_ref[0], t64_ref[1], t64_ref[2]
    y = proj(EV_R64)
    half = LANES // 2
    for blk in range((EV_R64[1] - EV_R64[0]) // LANES):
        r = _rope_block(y[:, LANES * blk:LANES * (blk + 1)], cos, s_hi, s_lo, IDX_DIM // 2)
        if blk < 4:
            r = (r * IDX_DIM ** -0.5).astype(BF16)
            iq_ref[2 * blk] = r[:, :half]
            iq_ref[2 * blk + 1] = r[:, half:]
        elif blk < 8:
            r = (r * (B_QK_DIM ** -0.5 * LOG2E)).astype(BF16)
            qb_ref[2 * (blk - 4)] = r[:, :half]
            qb_ref[2 * (blk - 4) + 1] = r[:, half:]
        elif blk < 12:
            r = r.astype(BF16)
            kb_ref[2 * (blk - 8)] = r[:, :half]
            kb_ref[2 * (blk - 8) + 1] = r[:, half:]
        else:
            ik_ref[...] = r[:, :half].astype(BF16)

    cos, s_hi, s_lo = t32_ref[0], t32_ref[1], t32_ref[2]
    y = proj(EV_R32)
    for blk in range(A_HEADS + 1):
        r = _rope_block(y[:, LANES * blk:LANES * (blk + 1)], cos, s_hi, s_lo, A_ROPE // 2)
        if blk < A_HEADS:
            qc_ref[blk, :, LANES:2 * LANES] = (r * (A_SCALE * LOG2E)).astype(BF16)
        else:
            kc_ref[:, LANES:2 * LANES] = r.astype(BF16)

    c_kv = _rms_rows(proj(EV_CKV), kvg_ref[...])
    kc_ref[:, 0:LANES] = c_kv.astype(BF16)
    cvt_ref[...] = c_kv.T.astype(BF16)
    vbt_ref[...] = proj(EV_VB).T.astype(BF16)
    iwt_ref[...] = (proj(EV_IW) * IDX_HEADS ** -0.5).T[0:IDX_HEADS, :]


def _even_proj(x2, gain, w, wuk_bd, kv_gain, t64, t32, seq):
    n = x2.shape[0]
    tm = PROJ_TM
    tiles_per_seq = seq // tm
    row = lambda i: (i, 0)
    col = lambda i: (0, i)
    hrow = lambda i: (0, i, 0)
    full2 = lambda i: (0, 0)
    tab = lambda i: (0, i % tiles_per_seq, 0)
    out_shape = (
        jax.ShapeDtypeStruct((A_HEADS, n, 2 * LANES), BF16),
        jax.ShapeDtypeStruct((n, 2 * LANES), BF16),
        jax.ShapeDtypeStruct((A_KV_RANK, n), BF16),
        jax.ShapeDtypeStruct((IDX_HEADS, n, IDX_DIM), BF16),
        jax.ShapeDtypeStruct((n, IDX_DIM), BF16),
        jax.ShapeDtypeStruct((IDX_HEADS, n), F32),
        jax.ShapeDtypeStruct((2 * B_HEADS, n, B_QK_DIM), BF16),
        jax.ShapeDtypeStruct((2 * B_HEADS, n, B_QK_DIM), BF16),
        jax.ShapeDtypeStruct((B_HEADS * B_V_DIM, n), BF16),
    )
    return pl.pallas_call(
        _even_proj_kernel,
        grid=(n // tm,),
        in_specs=[
            pl.BlockSpec((tm, D_MODEL), row),
            pl.BlockSpec((1, D_MODEL), full2),
            pl.BlockSpec((D_MODEL, EV_COLS), full2),
            pl.BlockSpec((A_HEADS * A_NOPE, A_HEADS * A_KV_RANK), full2),
            pl.BlockSpec((1, A_KV_RANK), full2),
            pl.BlockSpec((3, tm, LANES), tab),
            pl.BlockSpec((3, tm, LANES), tab),
        ],
        out_specs=(
            pl.BlockSpec((A_HEADS, tm, 2 * LANES), hrow),
            pl.BlockSpec((tm, 2 * LANES), row),
            pl.BlockSpec((A_KV_RANK, tm), col),
            pl.BlockSpec((IDX_HEADS, tm, IDX_DIM), hrow),
            pl.BlockSpec((tm, IDX_DIM), row),
            pl.BlockSpec((IDX_HEADS, tm), col),
            pl.BlockSpec((2 * B_HEADS, tm, B_QK_DIM), hrow),
            pl.BlockSpec((2 * B_HEADS, tm, B_QK_DIM), hrow),
            pl.BlockSpec((B_HEADS * B_V_DIM, tm), col),
        ),
        out_shape=out_shape,
        compiler_params=_cparams(("parallel",)),
        name="even_proj",
    )(x2, gain, w, wuk_bd, kv_gain, t64, t32)


def _dsa_kernel(iq_ref, ik_ref, iwt_ref, qc_ref, kc_ref, cvt_ref, o_ref,
                keys_ref, hi_ref, lo_ref, lo2_ref, jstar_ref, sa_ref, sb_ref, *, k_sel, seq):
    i = pl.program_id(1)
    qb, kc = A_QB, KEY_CHUNK
    n_chunks = (i * qb + qb + kc - 1) // kc
    t_lane = i * qb + lax.broadcasted_iota(jnp.int32, (1, qb), 1)
    sub_pos = lax.broadcasted_iota(jnp.int32, (kc, 1), 0)
    iwt = iwt_ref[...]

    iq = iq_ref[...].reshape(IDX_HEADS * qb, IDX_DIM)

    def index_chunk(c):
        k = ik_ref[pl.ds(pl.multiple_of(c * kc, kc), kc), :]
        rel_all = jnp.maximum(_dot_nt(k, iq), 0.0)
        score = None
        for hd in range(IDX_HEADS):
            rel = rel_all[:, qb * hd:qb * (hd + 1)] * iwt[hd:hd + 1, :]
            score = rel if score is None else score + rel
        bits = lax.bitcast_convert_type(score, jnp.int32)
        key = jnp.where(bits < 0, bits ^ 0x7FFFFFFF, bits)
        key = jnp.where(score == 0.0, 0, key)
        key = jnp.where(c * kc + sub_pos <= t_lane, key, KEY_NEG_INF)
        keys_ref[c] = key
        hi_ref[c] = (key >> 16).astype(jnp.int16)
        lo_ref[c] = ((key & 0xFFFF) - HALF16).astype(jnp.int16)

    def index_pair(pair, carry):
        index_chunk(2 * pair)
        index_chunk(jnp.minimum(2 * pair + 1, n_chunks - 1))
        return carry

    n_pairs = (n_chunks + 1) // 2
    lax.fori_loop(0, n_pairs, index_pair, 0)

    def count(pred):
        def body(c, acc):
            hit = jnp.where(pred(keys_ref[c], c), 1, 0)
            return acc + jnp.sum(hit.reshape(kc // SUBLANES, SUBLANES, qb), axis=0)
        acc = lax.fori_loop(0, n_chunks, body, jnp.zeros((SUBLANES, qb), jnp.int32))
        return jnp.sum(acc, axis=0, keepdims=True)

    pad_chunk = hi_ref.shape[0] - 1
    hi_ref[pad_chunk] = jnp.full((kc, qb), -HALF16, jnp.int16)
    lo2_ref[pad_chunk] = jnp.full((kc, qb), -HALF16, jnp.int16)
    rows16 = 2 * SUBLANES

    def count16(ref, pred):
        def hits(c):
            hit = jnp.where(pred(ref[c]), jnp.int16(1), jnp.int16(0))
            parts = [hit[rows16 * r:rows16 * (r + 1), :] for r in range(kc // rows16)]
            while len(parts) > 1:
                parts = [a + b for a, b in zip(parts[0::2], parts[1::2])]
            return parts[0]

        def body(pair, acc):
            second = jnp.where(2 * pair + 1 < n_chunks, 2 * pair + 1, pad_chunk)
            return acc + hits(2 * pair) + hits(second)
        acc = lax.fori_loop(0, n_pairs, body, jnp.zeros((rows16, qb), jnp.int16))
        return jnp.sum(acc.astype(jnp.int32), axis=0, keepdims=True)

    def kth_largest16(ref, k_row):
        u = jnp.where(count16(ref, lambda x: x >= jnp.int16(0)) >= k_row, 0, -HALF16)

        def bit(it, u):
            cand = u + jnp.left_shift(jnp.int32(1), 14 - it)
            c16 = cand.astype(jnp.int16)
            return jnp.where(count16(ref, lambda x: x >= c16) >= k_row, cand, u)
        return lax.fori_loop(0, 15, bit, u)

    v_hi = kth_largest16(hi_ref, k_sel)
    v_hi16 = v_hi.astype(jnp.int16)
    k_low = k_sel - count16(hi_ref, lambda x: x > v_hi16)

    def mask_low(c, carry):
        lo2_ref[c] = jnp.where(hi_ref[c] == v_hi16, lo_ref[c], jnp.int16(-HALF16))
        return carry

    lax.fori_loop(0, n_chunks, mask_low, 0)
    v_lo = kth_largest16(lo2_ref, k_low)
    v_lo16 = v_lo.astype(jnp.int16)
    v = v_hi * (2 * HALF16) + (v_lo + HALF16)
    count_ge = (k_sel - k_low) + count16(lo2_ref, lambda x: x >= v_lo16)

    jstar_ref[...] = jnp.full(jstar_ref.shape, seq - 1, jnp.int32)

    @pl.when(jnp.max(count_ge) > k_sel)
    def _():
        need = k_sel - count(lambda key, c: key > v)

        def index_bit(it, x):
            cand = x + jnp.left_shift(jnp.int32(1), (seq.bit_length() - 2) - it)
            below = count(lambda key, c: (key == v) & (c * kc + sub_pos < cand))
            return jnp.where(below < need, cand, x)

        x = lax.fori_loop(0, seq.bit_length() - 1, index_bit, jnp.zeros((1, qb), jnp.int32))
        jstar_ref[...] = jnp.broadcast_to(x, jstar_ref.shape)

    j_star = jstar_ref[0:1, :]

    q = qc_ref[...].reshape(A_HEADS * qb, 2 * LANES)

    def scores(c):
        cc = jnp.minimum(c, n_chunks - 1)
        kv = kc_ref[pl.ds(pl.multiple_of(cc * kc, kc), kc), :]
        key = keys_ref[cc]
        pos = cc * kc + sub_pos
        sel = ((key > v) | ((key == v) & (pos <= j_star))) & (pos <= t_lane) & (c < n_chunks)
        bias = jnp.where(sel, 0.0, MASKED)
        return _dot_nt(kv, q) + jnp.concatenate([bias] * A_HEADS, axis=1)

    def update(s_ref, c, state):
        cc = jnp.minimum(c, n_chunks - 1)
        return _flash_update(s_ref[...], cvt_ref[:, pl.ds(pl.multiple_of(cc * kc, kc), kc)], state)

    sa_ref[...] = scores(0)

    def attend(pair, state):
        sb_ref[...] = scores(2 * pair + 1)
        state = update(sa_ref, 2 * pair, state)
        sa_ref[...] = scores(2 * pair + 2)
        return update(sb_ref, 2 * pair + 1, state)

    state = lax.fori_loop(0, n_pairs, attend, _flash_init(A_KV_RANK, A_HEADS * qb))
    o = _flash_result(state, A_KV_RANK)
    for hd in range(A_HEADS):
        o_ref[:, A_KV_RANK * hd:A_KV_RANK * (hd + 1)] = o[:, qb * hd:qb * (hd + 1)].T.astype(BF16)


def _dsa(iq, ik, iwt, qc, kc, cvt, batch, seq):
    n = ik.shape[0]
    nq = seq // A_QB
    k_sel = min(TOPK_MAX, seq // 4)
    assert seq & (seq - 1) == 0 and seq % KEY_CHUNK == 0 and k_sel <= KEY_CHUNK and A_QB == LANES
    qrow = lambda b, i: (b * nq + i, 0)
    hqrow = lambda b, i: (0, b * nq + i, 0)
    brow = lambda b, i: (b, 0)
    return pl.pallas_call(
        functools.partial(_dsa_kernel, k_sel=k_sel, seq=seq),
        grid=(batch, nq),
        in_specs=[
            pl.BlockSpec((IDX_HEADS, A_QB, IDX_DIM), hqrow),
            pl.BlockSpec((seq, IDX_DIM), brow),
            pl.BlockSpec((IDX_HEADS, A_QB), lambda b, i: (0, b * nq + i)),
            pl.BlockSpec((A_HEADS, A_QB, 2 * LANES), hqrow),
            pl.BlockSpec((seq, 2 * LANES), brow),
            pl.BlockSpec((A_KV_RANK, seq), lambda b, i: (0, b)),
        ],
        out_specs=pl.BlockSpec((A_QB, A_HEADS * A_KV_RANK), qrow),
        out_shape=jax.ShapeDtypeStruct((n, A_HEADS * A_KV_RANK), BF16),
        scratch_shapes=[pltpu.VMEM((seq // KEY_CHUNK, KEY_CHUNK, A_QB), jnp.int32),
                        pltpu.VMEM((seq // KEY_CHUNK + 1, KEY_CHUNK, A_QB), jnp.int16),
                        pltpu.VMEM((seq // KEY_CHUNK, KEY_CHUNK, A_QB), jnp.int16),
                        pltpu.VMEM((seq // KEY_CHUNK + 1, KEY_CHUNK, A_QB), jnp.int16),
                        pltpu.VMEM((SUBLANES, A_QB), jnp.int32),
                        pltpu.VMEM((KEY_CHUNK, A_HEADS * A_QB), F32),
                        pltpu.VMEM((KEY_CHUNK, A_HEADS * A_QB), F32)],
        compiler_params=_cparams(("parallel", "arbitrary")),
        name="dsa",
    )(iq, ik, iwt, qc, kc, cvt)


def _diff_kernel(q_ref, k_ref, vt_ref, lam_ref, g_ref, o_ref, sa_ref, sb_ref, *, lam_init):
    i = pl.program_id(1)
    qb, kc = B_QB, KEY_CHUNK
    nsub = 2 * B_HEADS
    causal = (lax.broadcasted_iota(jnp.int32, (kc, qb), 0) <= lax.broadcasted_iota(jnp.int32, (kc, qb), 1))
    tri = jnp.where(causal, 0.0, MASKED)

    def scores(c):
        start = pl.multiple_of(jnp.minimum(c, i) * kc, kc)
        bias = jnp.where(c < i, 0.0, jnp.where(c == i, tri, MASKED))
        return jnp.concatenate([_dot_nt(k_ref[j, pl.ds(start, kc), :], q_ref[j]) + bias for j in range(nsub)], axis=1)

    def update(s_ref, c, states):
        start = pl.multiple_of(jnp.minimum(c, i) * kc, kc)
        return tuple(_flash_update(s_ref[:, 2 * qb * hd:2 * qb * (hd + 1)],
                                   vt_ref[B_V_DIM * hd:B_V_DIM * (hd + 1), pl.ds(start, kc)], states[hd])
                     for hd in range(B_HEADS))

    sa_ref[...] = scores(0)

    def sweep_pair(pair, states):
        sb_ref[...] = scores(2 * pair + 1)
        states = update(sa_ref, 2 * pair, states)
        sa_ref[...] = scores(2 * pair + 2)
        return update(sb_ref, 2 * pair + 1, states)

    states = lax.fori_loop(0, (i + 2) // 2, sweep_pair, tuple(_flash_init(B_V_DIM, 2 * qb) for _ in range(B_HEADS)))

    lam = lam_ref[...]
    lam_full = (jnp.exp(jnp.sum(lam[0:1] * lam[1:2], axis=-1, keepdims=True))
                - jnp.exp(jnp.sum(lam[2:3] * lam[3:4], axis=-1, keepdims=True)) + lam_init)
    gain_col = g_ref[...]
    for hd in range(B_HEADS):
        o = _flash_result(states[hd], B_V_DIM)
        d = o[:, 0:qb] - lam_full * o[:, qb:2 * qb]
        d = d * lax.rsqrt(jnp.mean(d * d, axis=0, keepdims=True) + NORM_EPS) * gain_col * (1.0 - lam_init)
        o_ref[:, B_V_DIM * hd:B_V_DIM * (hd + 1)] = d.T.astype(BF16)


def _diff(qb, kb, vbt, lam, subln, lam_init, batch, seq):
    n = vbt.shape[1]
    nq = seq // B_QB
    assert B_QB == KEY_CHUNK
    return pl.pallas_call(
        functools.partial(_diff_kernel, lam_init=lam_init),
        grid=(batch, nq),
        in_specs=[
            pl.BlockSpec((2 * B_HEADS, B_QB, B_QK_DIM), lambda b, i: (0, b * nq + i, 0)),
            pl.BlockSpec((2 * B_HEADS, seq, B_QK_DIM), lambda b, i: (0, b, 0)),
            pl.BlockSpec((B_HEADS * B_V_DIM, seq), lambda b, i: (0, b)),
            pl.BlockSpec((4, B_QK_DIM), lambda b, i: (0, 0)),
            pl.BlockSpec((B_V_DIM, 1), lambda b, i: (0, 0)),
        ],
        out_specs=pl.BlockSpec((B_QB, B_HEADS * B_V_DIM), lambda b, i: (b * nq + i, 0)),
        out_shape=jax.ShapeDtypeStruct((n, B_HEADS * B_V_DIM), BF16),
        scratch_shapes=[pltpu.VMEM((KEY_CHUNK, 2 * B_HEADS * B_QB), F32), pltpu.VMEM((KEY_CHUNK, 2 * B_HEADS * B_QB), F32)],
        compiler_params=_cparams(("parallel", "arbitrary")),
        name="diff_attn",
    )(qb, kb, vbt, lam, subln)


def _even_out_kernel(x_ref, ol_ref, ob_ref, wuv_ref, wo_ref, y_ref):
    half = A_HEADS * A_V_DIM
    o_a = _dot(ol_ref[...], wuv_ref[...]).astype(BF16)
    y_ref[...] = x_ref[...] + _dot(o_a, wo_ref[0:half, :]) + _dot(ob_ref[...], wo_ref[half:, :])


def _even_out(x2, o_lat, o_b, wuv_bd, w_out):
    n = x2.shape[0]
    tm = PROJ_TM
    row = lambda i: (i, 0)
    full2 = lambda i: (0, 0)
    return pl.pallas_call(
        _even_out_kernel,
        grid=(n // tm,),
        in_specs=[
            pl.BlockSpec((tm, D_MODEL), row),
            pl.BlockSpec((tm, A_HEADS * A_KV_RANK), row),
            pl.BlockSpec((tm, B_HEADS * B_V_DIM), row),
            pl.BlockSpec((A_HEADS * A_KV_RANK, A_HEADS * A_V_DIM), full2),
            pl.BlockSpec((D_MODEL, D_MODEL), full2),
        ],
        out_specs=pl.BlockSpec((tm, D_MODEL), row),
        out_shape=jax.ShapeDtypeStruct((n, D_MODEL), F32),
        compiler_params=_cparams(("parallel",)),
        name="even_out",
    )(x2, o_lat, o_b, wuv_bd, w_out)


OD_R64 = (0, 1792)
OD_V = (1792, 2560)
OD_G = (2560, 3072)
OD_COLS = 3072
GATES_PER_GROUP = C_HPG * 3
KV_WIDTH = C_GROUPS * C_DIM


def _odd_weight(w_in):
    qc, kc, vc, ks, vs, kw, vw, gc = jnp.split(w_in, np.cumsum(ODD_SPLITS)[:-1].tolist(), axis=1)
    d = w_in.shape[0]
    gates = jnp.pad(gc.reshape(d, C_GROUPS, GATES_PER_GROUP), ((0, 0), (0, 0), (0, LANES - GATES_PER_GROUP)))
    w = jnp.concatenate([qc, kc, ks, kw, vc, vs, vw, gates.reshape(d, C_GROUPS * LANES)], axis=1)
    assert w.shape[1] == OD_COLS
    return w.astype(BF16)


def _odd_proj_kernel(x_ref, g_ref, w_ref, t64_ref, q_ref, kc_ref, ks_ref, kw_ref, vc_ref, vst_ref, vwt_ref, gt_ref):
    h = _rms_rows(x_ref[...], g_ref[...]).astype(BF16)
    half = LANES // 2
    cos, s_hi, s_lo = t64_ref[0], t64_ref[1], t64_ref[2]
    y = _dot(h, w_ref[:, OD_R64[0]:OD_R64[1]])
    for blk in range((OD_R64[1] - OD_R64[0]) // LANES):
        r = _rope_block(y[:, LANES * blk:LANES * (blk + 1)], cos, s_hi, s_lo, C_DIM // 2)
        if blk < 8:
            r = (r * (C_DIM ** -0.5 * LOG2E)).astype(BF16)
            dst, j = q_ref, blk
        else:
            r = r.astype(BF16)
            dst, j = (kc_ref, ks_ref, kw_ref)[(blk - 8) // 2], (blk - 8) % 2
        dst[2 * j] = r[:, :half]
        dst[2 * j + 1] = r[:, half:]
    y = _dot(h, w_ref[:, OD_V[0]:OD_V[1]])
    vc = y[:, 0:KV_WIDTH].astype(BF16)
    for g in range(C_GROUPS):
        vc_ref[g] = vc[:, C_DIM * g:C_DIM * (g + 1)]
    vst_ref[...] = y[:, KV_WIDTH:2 * KV_WIDTH].T.astype(BF16)
    vwt_ref[...] = y[:, 2 * KV_WIDTH:3 * KV_WIDTH].T.astype(BF16)
    y = jax.nn.sigmoid(_dot(h, w_ref[:, OD_G[0]:OD_G[1]]))
    for g in range(C_GROUPS):
        gt_ref[g] = y[:, LANES * g:LANES * (g + 1)]


def _odd_proj(x2, gain, w, t64, seq):
    n = x2.shape[0]
    tm = PROJ_TM
    tiles_per_seq = seq // tm
    row = lambda i: (i, 0)
    col = lambda i: (0, i)
    hrow = lambda i: (0, i, 0)
    full2 = lambda i: (0, 0)
    kv_shape = jax.ShapeDtypeStruct((C_GROUPS, n, C_DIM), BF16)
    kv_spec = pl.BlockSpec((C_GROUPS, tm, C_DIM), hrow)
    kvt_shape = jax.ShapeDtypeStruct((KV_WIDTH, n), BF16)
    kvt_spec = pl.BlockSpec((KV_WIDTH, tm), col)
    return pl.pallas_call(
        _odd_proj_kernel,
        grid=(n // tm,),
        in_specs=[
            pl.BlockSpec((tm, D_MODEL), row),
            pl.BlockSpec((1, D_MODEL), full2),
            pl.BlockSpec((D_MODEL, OD_COLS), full2),
            pl.BlockSpec((3, tm, LANES), lambda i: (0, i % tiles_per_seq, 0)),
        ],
        out_specs=(pl.BlockSpec((C_HEADS, tm, C_DIM), hrow),) + (kv_spec,) * 4 + (kvt_spec,) * 2
        + (pl.BlockSpec((C_GROUPS, tm, LANES), hrow),),
        out_shape=(jax.ShapeDtypeStruct((C_HEADS, n, C_DIM), BF16),) + (kv_shape,) * 4 + (kvt_shape,) * 2
        + (jax.ShapeDtypeStruct((C_GROUPS, n, LANES), F32),),
        compiler_params=_cparams(("parallel",)),
        name="odd_proj",
    )(x2, gain, w, t64)


def _gelu_tanh(x):
    return 0.5 * x * (1.0 + jnp.tanh(math.sqrt(2.0 / math.pi) * (x + 0.044715 * (x * x * x))))


def _compress_kernel(kch_ref, vch_ref, pe_ref, w1_ref, w2_ref, w2t_ref, kc_ref, vct_ref):
    rows = kch_ref.shape[2]

    def hidden(src, kv):
        ch = src[0, 0].astype(F32)
        first = _dot((ch + pe_ref[kv, 0:1, :]).astype(BF16), w1_ref[kv, 0])
        second = _dot((ch + pe_ref[kv, 1:2, :]).astype(BF16), w1_ref[kv, 1])
        return _gelu_tanh(first + pltpu.roll(second, rows - 1, 0)).astype(BF16)

    kc_ref[0, 0] = _dot(hidden(kch_ref, 0), w2_ref[...]).astype(BF16)
    vct_ref[0, 0] = _dot_nt(w2t_ref[...], hidden(vch_ref, 1)).astype(BF16)


def _compress(kc_raw, vc_raw, pe, w1, w2, batch, seq):
    nchunk = seq // CMP_STRIDE
    width = CMP_STRIDE * C_DIM
    kch = kc_raw.reshape(C_GROUPS, batch, nchunk, width)
    vch = vc_raw.reshape(C_GROUPS, batch, nchunk, width)
    pe2 = pe.reshape(2, 2, width)
    w1s = w1.reshape(2, 2, width, CMP_HIDDEN).astype(BF16)
    blk = lambda g, b: (g, b, 0, 0)
    full2 = lambda g, b: (0, 0)
    return pl.pallas_call(
        _compress_kernel,
        grid=(C_GROUPS, batch),
        in_specs=[
            pl.BlockSpec((1, 1, nchunk, width), blk),
            pl.BlockSpec((1, 1, nchunk, width), blk),
            pl.BlockSpec((2, 2, width), lambda g, b: (0, 0, 0)),
            pl.BlockSpec((2, 2, width, CMP_HIDDEN), lambda g, b: (0, 0, 0, 0)),
            pl.BlockSpec((CMP_HIDDEN, C_DIM), full2),
            pl.BlockSpec((C_DIM, CMP_HIDDEN), full2),
        ],
        out_specs=(pl.BlockSpec((1, 1, nchunk, C_DIM), blk), pl.BlockSpec((1, 1, C_DIM, nchunk), blk)),
        out_shape=(jax.ShapeDtypeStruct((C_GROUPS, batch, nchunk, C_DIM), BF16),
                   jax.ShapeDtypeStruct((C_GROUPS, batch, C_DIM, nchunk), BF16)),
        compiler_params=_cparams(("parallel", "parallel")),
        name="nsa_compress",
    )(kch, vch, pe2, w1s, w2[0].astype(BF16), w2[1].T.astype(BF16))


def _split3(x):
    a = x.astype(BF16)
    r = x - a.astype(F32)
    b = r.astype(BF16)
    c = (r - b.astype(F32)).astype(BF16)
    return a, b, c


def _nsa_kernel(q_ref, kc_ref, vct_ref, ks_ref, vst_ref, kw_ref, vwt_ref, gt_ref, o_ref, sa_ref, sb_ref, sc_ref,
                *, n_blk, top_n):
    i = pl.program_id(2)
    qb, kc = C_QB, KEY_CHUNK
    q = q_ref[...].reshape(C_HPG * qb, C_DIM)
    heads = lambda a: jnp.concatenate([a] * C_HPG, axis=1)
    t_lane = i * qb + lax.broadcasted_iota(jnp.int32, (1, qb), 1)
    sub_pos = lax.broadcasted_iota(jnp.int32, (kc, 1), 0)

    n_cmp = kc_ref.shape[2]
    cmp_end = lax.broadcasted_iota(jnp.int32, (n_cmp, 1), 0) * CMP_STRIDE + (CMP_LEN - 1)
    vis = heads(jnp.where(cmp_end <= t_lane, 1.0, 0.0))
    s = _dot_nt(kc_ref[0, 0], q) + (vis - 1.0) * (-MASKED)
    e = jnp.exp2(s - jnp.max(s, axis=0, keepdims=True)) * vis
    den = jnp.sum(e, axis=0, keepdims=True)
    p_cmp = e / jnp.where(den > 0, den, 1.0)
    o_cmp = _dot(vct_ref[0, 0], p_cmp.astype(BF16))

    p_sum = p_cmp[:, 0:qb]
    for hd in range(1, C_HPG):
        p_sum = p_sum + p_cmp[:, qb * hd:qb * (hd + 1)]
    blk_id = lax.broadcasted_iota(jnp.int32, (n_blk, n_cmp), 0)
    cmp_id = lax.broadcasted_iota(jnp.int32, (n_blk, n_cmp), 1)
    ratio = SEL_BLOCK // CMP_STRIDE
    overlap_t = ((cmp_id < ratio * (blk_id + 1)) & (cmp_id + CMP_LEN // CMP_STRIDE > ratio * blk_id))
    overlap_t = jnp.where(overlap_t, 1.0, 0.0).astype(BF16)
    imp = sum(_dot(overlap_t, part) for part in _split3(p_sum))
    j_sub = lax.broadcasted_iota(jnp.int32, (n_blk, 1), 0)
    cur = t_lane // SEL_BLOCK
    forced = (j_sub == 0) | (j_sub == cur) | (j_sub == cur - 1)
    imp = jnp.where(forced, jnp.inf, imp)
    imp = jnp.where(j_sub * SEL_BLOCK <= t_lane, imp, -jnp.inf)
    ranks = []
    for tile in range(n_blk // SUBLANES):
        mine = imp[SUBLANES * tile:SUBLANES * (tile + 1), :]
        j_tile = j_sub[SUBLANES * tile:SUBLANES * (tile + 1), :]
        rank = jnp.zeros((SUBLANES, qb), jnp.int32)
        for other in range(n_blk):
            row = imp[other:other + 1, :]
            ge = jnp.where(row >= mine, 1, 0)
            gt = jnp.where(row > mine, 1, 0)
            if other < SUBLANES * tile:
                rank = rank + ge
            elif other >= SUBLANES * (tile + 1):
                rank = rank + gt
            else:
                rank = rank + jnp.where(j_tile > other, ge, gt)
        ranks.append(rank)
    sel = jnp.where(jnp.concatenate(ranks, axis=0) < top_n, 1.0, 0.0).astype(BF16)

    init = _flash_init(C_DIM, C_HPG * qb)

    key_blk = lax.broadcasted_iota(jnp.int32, (kc, n_blk), 0) // SEL_BLOCK
    blk_col = lax.broadcasted_iota(jnp.int32, (kc, n_blk), 1)

    def slc_scores(c):
        cc = jnp.minimum(c, i)
        expand = jnp.where(blk_col == key_blk + cc * (kc // SEL_BLOCK), 1.0, 0.0).astype(BF16)
        allowed = (_dot(expand, sel) > 0.5) & (cc * kc + sub_pos <= t_lane) & (c <= i)
        k = ks_ref[0, pl.ds(pl.multiple_of(cc * kc, kc), kc), :]
        return _dot_nt(k, q) + heads(jnp.where(allowed, 0.0, MASKED))

    def slc_update(s_ref, c, state):
        cc = jnp.minimum(c, i)
        return _flash_update(s_ref[...], vst_ref[:, pl.ds(pl.multiple_of(cc * kc, kc), kc)], state)

    sa_ref[...] = slc_scores(0)

    def slc_pair(pair, state):
        sb_ref[...] = slc_scores(2 * pair + 1)
        state = slc_update(sa_ref, 2 * pair, state)
        sa_ref[...] = slc_scores(2 * pair + 2)
        return slc_update(sb_ref, 2 * pair + 1, state)

    o_slc = _flash_result(lax.fori_loop(0, (i + 2) // 2, slc_pair, init), C_DIM)

    assert qb == kc and WINDOW % kc == 0
    back = WINDOW // kc
    assert back + 1 == 3
    key_i = lax.broadcasted_iota(jnp.int32, (kc, qb), 0)
    lane_i = lax.broadcasted_iota(jnp.int32, (kc, qb), 1)
    stage = (sa_ref, sb_ref, sc_ref)
    for step in range(back + 1):
        c = i - back + step
        exists = jnp.where(c >= 0, 0.0, MASKED)
        if step == 0:
            bias = jnp.where(key_i > lane_i, exists, MASKED)
        elif step < back:
            bias = jnp.where(key_i >= 0, exists, MASKED)
        else:
            bias = jnp.where(key_i <= lane_i, 0.0, MASKED)
        start = pl.multiple_of(jnp.maximum(c, 0) * kc, kc)
        stage[step][...] = _dot_nt(kw_ref[0, pl.ds(start, kc), :], q) + heads(bias)
    win = init
    for step in range(back + 1):
        start = pl.multiple_of(jnp.maximum(i - back + step, 0) * kc, kc)
        win = _flash_update(stage[step][...], vwt_ref[:, pl.ds(start, kc)], win)
    o_win = _flash_result(win, C_DIM)

    gates_t = gt_ref[0].T
    cols = []
    for hd in range(C_HPG):
        g0, g1, g2 = (gates_t[3 * hd + j:3 * hd + j + 1, :] for j in range(3))
        blk = slice(qb * hd, qb * (hd + 1))
        cols.append(g0 * o_cmp[:, blk] + g1 * o_slc[:, blk] + g2 * o_win[:, blk])
    o_ref[...] = jnp.concatenate(cols, axis=0).T.astype(BF16)


def _nsa(q, k_cmp, v_cmp_t, ks, vst, kw, vwt, gates, batch, seq):
    n = ks.shape[1]
    nq = seq // C_QB
    n_blk = seq // SEL_BLOCK
    top_n = min(SEL_TOPN, n_blk)
    nchunk = seq // CMP_STRIDE
    k_spec = pl.BlockSpec((1, seq, C_DIM), lambda b, g, i: (g, b, 0))
    vt_spec = pl.BlockSpec((C_DIM, seq), lambda b, g, i: (g, b))
    return pl.pallas_call(
        functools.partial(_nsa_kernel, n_blk=n_blk, top_n=top_n),
        grid=(batch, C_GROUPS, nq),
        in_specs=[
            pl.BlockSpec((C_HPG, C_QB, C_DIM), lambda b, g, i: (g, b * nq + i, 0)),
            pl.BlockSpec((1, 1, nchunk, C_DIM), lambda b, g, i: (g, b, 0, 0)),
            pl.BlockSpec((1, 1, C_DIM, nchunk), lambda b, g, i: (g, b, 0, 0)),
            k_spec, vt_spec, k_spec, vt_spec,
            pl.BlockSpec((1, C_QB, LANES), lambda b, g, i: (g, b * nq + i, 0)),
        ],
        out_specs=pl.BlockSpec((C_QB, C_HPG * C_DIM), lambda b, g, i: (b * nq + i, g)),
        out_shape=jax.ShapeDtypeStruct((n, C_HEADS * C_DIM), BF16),
        scratch_shapes=[pltpu.VMEM((KEY_CHUNK, C_HPG * C_QB), F32)] * 3,
        compiler_params=_cparams(("parallel", "parallel", "arbitrary")),
        name="nsa",
    )(q, k_cmp, v_cmp_t, ks, vst, kw, vwt, gates)


def _odd_out_kernel(x_ref, o_ref, wo_ref, y_ref):
    y_ref[...] = x_ref[...] + _dot(o_ref[...], wo_ref[...])


def _odd_out(x2, o_c, w_out):
    n = x2.shape[0]
    tm = PROJ_TM
    row = lambda i: (i, 0)
    return pl.pallas_call(
        _odd_out_kernel,
        grid=(n // tm,),
        in_specs=[
            pl.BlockSpec((tm, D_MODEL), row),
            pl.BlockSpec((tm, D_MODEL), row),
            pl.BlockSpec((D_MODEL, D_MODEL), lambda i: (0, 0)),
        ],
        out_specs=pl.BlockSpec((tm, D_MODEL), row),
        out_shape=jax.ShapeDtypeStruct((n, D_MODEL), F32),
        compiler_params=_cparams(("parallel",)),
        name="odd_out",
    )(x2, o_c, w_out)


FFN_CHUNKS = 2
FFN_SUBCHUNKS = 1


def _ffn_kernel(x_ref, g_ref, wg_ref, wu_ref, wd_ref, gf_ref, y_ref, h_ref, acc_ref, *, final_norm):
    f = pl.program_id(1)

    @pl.when(f == 0)
    def _():
        h_ref[...] = _rms_rows(x_ref[...], g_ref[...]).astype(BF16)

    h = h_ref[...]
    part = None
    sub = wg_ref.shape[1] // FFN_SUBCHUNKS
    for j in range(FFN_SUBCHUNKS):
        gate = _dot(h, wg_ref[:, sub * j:sub * (j + 1)])
        act = (gate * jax.nn.sigmoid(gate) * _dot(h, wu_ref[:, sub * j:sub * (j + 1)])).astype(BF16)
        down = _dot(act, wd_ref[sub * j:sub * (j + 1), :])
        part = down if part is None else part + down

    @pl.when(f == 0)
    def _():
        acc_ref[...] = part

    @pl.when(f == FFN_CHUNKS - 1)
    def _():
        y = x_ref[...] + acc_ref[...] + part if FFN_CHUNKS > 1 else x_ref[...] + part
        y_ref[...] = _rms_rows(y, gf_ref[...]) if final_norm else y


def _ffn(x2, gain, wg, wu, wd, g_final, final_norm):
    n = x2.shape[0]
    tm = PROJ_TM
    fc = D_FF // FFN_CHUNKS
    assert FFN_CHUNKS in (1, 2) and fc % LANES == 0
    row = lambda i, f: (i, 0)
    full2 = lambda i, f: (0, 0)
    return pl.pallas_call(
        functools.partial(_ffn_kernel, final_norm=final_norm),
        grid=(n // tm, FFN_CHUNKS),
        in_specs=[
            pl.BlockSpec((tm, D_MODEL), row),
            pl.BlockSpec((1, D_MODEL), full2),
            pl.BlockSpec((D_MODEL, fc), lambda i, f: (0, f)),
            pl.BlockSpec((D_MODEL, fc), lambda i, f: (0, f)),
            pl.BlockSpec((fc, D_MODEL), lambda i, f: (f, 0)),
            pl.BlockSpec((1, D_MODEL), full2),
        ],
        out_specs=pl.BlockSpec((tm, D_MODEL), row),
        out_shape=jax.ShapeDtypeStruct((n, D_MODEL), F32),
        scratch_shapes=[pltpu.VMEM((tm, D_MODEL), BF16), pltpu.VMEM((tm, D_MODEL), F32)],
        compiler_params=_cparams(("parallel", "arbitrary")),
        name="ffn",
    )(x2, gain, wg, wu, wd, g_final)


def kernel(x, norm_mix, norm_ffn, norm_final, ev_w_in, ev_kv_gain, ev_w_uk, ev_w_uv, ev_lambda, ev_subln, ev_w_out,
           od_w_in, od_cmp_pe, od_cmp_w1, od_cmp_w2, od_w_out, ffn_w_gate, ffn_w_up, ffn_w_down):
    batch, seq, d = x.shape
    depth = norm_mix.shape[0]
    x2 = x.reshape(batch * seq, d)
    t64 = _rope_tables(seq, 64)
    t32 = _rope_tables(seq, A_ROPE)
    g_final = norm_final.reshape(1, d)
    for layer in range(depth):
        j = layer // 2
        gain = norm_mix[layer].reshape(1, d)
        if layer % 2 == 0:
            qc, kc, cvt, iq, ik, iwt, qb, kb, vbt = _even_proj(
                x2, gain, _even_weight(ev_w_in[j]), _block_diag(ev_w_uk[j]).astype(BF16),
                ev_kv_gain[j].reshape(1, A_KV_RANK), t64, t32, seq)
            o_lat = _dsa(iq, ik, iwt, qc, kc, cvt, batch, seq)
            lam_init = 0.8 - 0.6 * math.exp(-0.3 * layer)
            o_b = _diff(qb, kb, vbt, ev_lambda[j], ev_subln[j].reshape(B_V_DIM, 1), lam_init, batch, seq)
            x2 = _even_out(x2, o_lat, o_b, _block_diag(ev_w_uv[j]).astype(BF16), ev_w_out[j].astype(BF16))
        else:
            q, kc_raw, ks, kw, vc_raw, vst, vwt, gates = _odd_proj(x2, gain, _odd_weight(od_w_in[j]), t64, seq)
            k_cmp, v_cmp_t = _compress(kc_raw, vc_raw, od_cmp_pe[j], od_cmp_w1[j], od_cmp_w2[j], batch, seq)
            o_c = _nsa(q, k_cmp, v_cmp_t, ks, vst, kw, vwt, gates, batch, seq)
            x2 = _odd_out(x2, o_c, od_w_out[j].astype(BF16))
        x2 = _ffn(x2, norm_ffn[layer].reshape(1, d), ffn_w_gate[layer].astype(BF16), ffn_w_up[layer].astype(BF16),
                  ffn_w_down[layer].astype(BF16), g_final, layer == depth - 1)
    return x2.reshape(batch, seq, d)
```

```python
import functools
import math

import numpy as np
import jax
import jax.numpy as jnp
from jax import lax
from jax.experimental import pallas as pl
from jax.experimental.pallas import tpu as pltpu

F32 = jnp.float32
BF16 = jnp.bfloat16

D_MODEL = 1024
ROPE_THETA = 10000.0
NORM_EPS = 1e-6

A_HEADS = 8
A_NOPE = 64
A_ROPE = 32
A_KV_RANK = 128
A_V_DIM = 64
A_SCALE = (A_NOPE + A_ROPE) ** -0.5
IDX_HEADS = 8
IDX_DIM = 64
TOPK_MAX = 256

B_HEADS = 4
B_QK_DIM = 64
B_V_DIM = 2 * B_QK_DIM

C_HEADS = 16
C_GROUPS = 4
C_HPG = C_HEADS // C_GROUPS
C_DIM = 64
CMP_LEN = 32
CMP_STRIDE = 16
CMP_HIDDEN = 128
SEL_BLOCK = 64
SEL_TOPN = 16
WINDOW = 512

D_FF = -(-8 * D_MODEL // (3 * 256)) * 256

EVEN_SPLITS = [A_HEADS * A_NOPE, A_HEADS * A_ROPE, A_KV_RANK, A_ROPE, IDX_HEADS * IDX_DIM, IDX_DIM, IDX_HEADS,
               B_HEADS * 2 * B_QK_DIM, B_HEADS * 2 * B_QK_DIM, B_HEADS * B_V_DIM]
ODD_SPLITS = [C_HEADS * C_DIM] + [C_GROUPS * C_DIM] * 6 + [C_HEADS * 3]

LANES = 128
SUBLANES = 8
VMEM_LIMIT = 56 * 1024 * 1024
MASKED = -1e30
KEY_NEG_INF = -2139095041
HALF16 = 1 << 15
LOG2E = math.log2(math.e)

PROJ_TM = 512
A_QB = 256
ATTEND_LANES = 512
KEY_CHUNK = 256
B_QB = 256
C_QB = 256


def _cparams(sem):
    return pltpu.CompilerParams(dimension_semantics=sem, vmem_limit_bytes=VMEM_LIMIT)


def _dot(a, b):
    return jnp.dot(a, b, preferred_element_type=F32)


def _dot_nt(a, b):
    return lax.dot_general(a, b, (((1,), (1,)), ((), ())), preferred_element_type=F32)


def _rms_rows(x, gain):
    return x * lax.rsqrt(jnp.mean(x * x, axis=-1, keepdims=True) + NORM_EPS) * gain


def _rope_block(y, cos, s_hi, s_lo, half):
    return y * cos + pltpu.roll(y, half, 1) * s_hi + pltpu.roll(y, LANES - half, 1) * s_lo


def _rope_tables(seq, d):
    half = d // 2
    pos = jnp.arange(seq, dtype=jnp.int32)
    inv = ROPE_THETA ** (-jnp.arange(0, d, 2, dtype=F32) / d)
    ang = pos.astype(F32)[:, None] * inv[None, :]
    cos, sin = jnp.cos(ang), jnp.sin(ang)
    lane = np.arange(LANES)
    idx = (lane % d) % half
    first = jnp.asarray((lane % d) < half)
    c = cos[:, idx]
    s = sin[:, idx]
    return jnp.stack([c, jnp.where(first[None, :], 0.0, s), jnp.where(first[None, :], -s, 0.0)])


ONES_ROWS = 16


def _flash_update(s, vt, state):
    m, acc = state
    m_new = jnp.maximum(m, jnp.max(s, axis=0, keepdims=True))
    alpha = jnp.exp2(m - m_new)
    p = jnp.exp2(s - m_new).astype(BF16)
    vt_ones = jnp.concatenate([vt, jnp.ones((ONES_ROWS, vt.shape[1]), BF16)], axis=0)
    return m_new, alpha * acc + _dot(vt_ones, p)


def _flash_init(dv, lanes):
    return (jnp.full((1, lanes), MASKED, F32), jnp.zeros((dv + ONES_ROWS, lanes), F32))


def _flash_result(state, dv):
    _, acc = state
    return acc[0:dv] / acc[dv:dv + 1]


EV_QN = (0, 512)
EV_R64 = (512, 2176)
EV_R32 = (2176, 3328)
EV_CKV = (3328, 3456)
EV_VB = (3456, 3968)
EV_IW = (3968, 4096)
EV_COLS = 4096


def _even_weight(w_in):
    qa_nope, qa_rope, c_kv, ka_rope, iq, ik, iw, qb, kb, vb = jnp.split(w_in, np.cumsum(EVEN_SPLITS)[:-1].tolist(), axis=1)
    d = w_in.shape[0]
    z = lambda n: jnp.zeros((d, n), w_in.dtype)
    qr = jnp.pad(qa_rope.reshape(d, A_HEADS, A_ROPE), ((0, 0), (0, 0), (0, LANES - A_ROPE))).reshape(d, A_HEADS * LANES)
    cols = [qa_nope, iq, qb, kb, ik, z(LANES - IDX_DIM), qr, ka_rope, z(LANES - A_ROPE), c_kv, vb, iw, z(LANES - IDX_HEADS)]
    w = jnp.concatenate(cols, axis=1)
    assert w.shape[1] == EV_COLS
    return w.astype(BF16)


def _block_diag(w):
    h, a, b = w.shape
    eye = jnp.eye(h, dtype=w.dtype)
    return (eye[:, None, :, None] * w[:, :, None, :]).reshape(h * a, h * b)


def _even_proj_kernel(x_ref, g_ref, w_ref, wuk_ref, kvg_ref, t64_ref, t32_ref,
                      qc_ref, kc_ref, cvt_ref, iq_ref, ik_ref, iwt_ref, qb_ref, kb_ref, vbt_ref):
    h = _rms_rows(x_ref[...], g_ref[...]).astype(BF16)

    def proj(seg):
        return _dot(h, w_ref[:, seg[0]:seg[1]])

    q_lat = _dot(proj(EV_QN).astype(BF16), wuk_ref[...]) * (A_SCALE * LOG2E)
    for hd in range(A_HEADS):
        qc_ref[hd, :, 0:LANES] = q_lat[:, LANES * hd:LANES * (hd + 1)].astype(BF16)

    cos, s_hi, s_lo = t64_ref[0], t64_ref[1], t64_ref[2]
    y = proj(EV_R64)
    half = LANES // 2
    for blk in range((EV_R64[1] - EV_R64[0]) // LANES):
        r = _rope_block(y[:, LANES * blk:LANES * (blk + 1)], cos, s_hi, s_lo, IDX_DIM // 2)
        if blk < 4:
            r = (r * IDX_DIM ** -0.5).astype(BF16)
            iq_ref[2 * blk] = r[:, :half]
            iq_ref[2 * blk + 1] = r[:, half:]
        elif blk < 8:
            r = (r * (B_QK_DIM ** -0.5 * LOG2E)).astype(BF16)
            qb_ref[2 * (blk - 4)] = r[:, :half]
            qb_ref[2 * (blk - 4) + 1] = r[:, half:]
        elif blk < 12:
            r = r.astype(BF16)
            kb_ref[2 * (blk - 8)] = r[:, :half]
            kb_ref[2 * (blk - 8) + 1] = r[:, half:]
        else:
            ik_ref[...] = r[:, :half].astype(BF16)

    cos, s_hi, s_lo = t32_ref[0], t32_ref[1], t32_ref[2]
    y = proj(EV_R32)
    for blk in range(A_HEADS + 1):
        r = _rope_block(y[:, LANES * blk:LANES * (blk + 1)], cos, s_hi, s_lo, A_ROPE // 2)
        if blk < A_HEADS:
            qc_ref[blk, :, LANES:2 * LANES] = (r * (A_SCALE * LOG2E)).astype(BF16)
        else:
            kc_ref[:, LANES:2 * LANES] = r.astype(BF16)

    c_kv = _rms_rows(proj(EV_CKV), kvg_ref[...])
    kc_ref[:, 0:LANES] = c_kv.astype(BF16)
    cvt_ref[...] = c_kv.T.astype(BF16)
    vbt_ref[...] = proj(EV_VB).T.astype(BF16)
    iwt_ref[...] = (proj(EV_IW) * IDX_HEADS ** -0.5).T[0:IDX_HEADS, :]


def _even_proj(x2, gain, w, wuk_bd, kv_gain, t64, t32, seq):
    n = x2.shape[0]
    tm = PROJ_TM
    tiles_per_seq = seq // tm
    row = lambda i: (i, 0)
    col = lambda i: (0, i)
    hrow = lambda i: (0, i, 0)
    full2 = lambda i: (0, 0)
    tab = lambda i: (0, i % tiles_per_seq, 0)
    out_shape = (
        jax.ShapeDtypeStruct((A_HEADS, n, 2 * LANES), BF16),
        jax.ShapeDtypeStruct((n, 2 * LANES), BF16),
        jax.ShapeDtypeStruct((A_KV_RANK, n), BF16),
        jax.ShapeDtypeStruct((IDX_HEADS, n, IDX_DIM), BF16),
        jax.ShapeDtypeStruct((n, IDX_DIM), BF16),
        jax.ShapeDtypeStruct((IDX_HEADS, n), F32),
        jax.ShapeDtypeStruct((2 * B_HEADS, n, B_QK_DIM), BF16),
        jax.ShapeDtypeStruct((2 * B_HEADS, n, B_QK_DIM), BF16),
        jax.ShapeDtypeStruct((B_HEADS * B_V_DIM, n), BF16),
    )
    return pl.pallas_call(
        _even_proj_kernel,
        grid=(n // tm,),
        in_specs=[
            pl.BlockSpec((tm, D_MODEL), row),
            pl.BlockSpec((1, D_MODEL), full2),
            pl.BlockSpec((D_MODEL, EV_COLS), full2),
            pl.BlockSpec((A_HEADS * A_NOPE, A_HEADS * A_KV_RANK), full2),
            pl.BlockSpec((1, A_KV_RANK), full2),
            pl.BlockSpec((3, tm, LANES), tab),
            pl.BlockSpec((3, tm, LANES), tab),
        ],
        out_specs=(
            pl.BlockSpec((A_HEADS, tm, 2 * LANES), hrow),
            pl.BlockSpec((tm, 2 * LANES), row),
            pl.BlockSpec((A_KV_RANK, tm), col),
            pl.BlockSpec((IDX_HEADS, tm, IDX_DIM), hrow),
            pl.BlockSpec((tm, IDX_DIM), row),
            pl.BlockSpec((IDX_HEADS, tm), col),
            pl.BlockSpec((2 * B_HEADS, tm, B_QK_DIM), hrow),
            pl.BlockSpec((2 * B_HEADS, tm, B_QK_DIM), hrow),
            pl.BlockSpec((B_HEADS * B_V_DIM, tm), col),
        ),
        out_shape=out_shape,
        compiler_params=_cparams(("parallel",)),
        name="even_proj",
    )(x2, gain, w, wuk_bd, kv_gain, t64, t32)


def _dsa_kernel(iq_ref, ik_ref, iwt_ref, qc_ref, kc_ref, cvt_ref, o_ref,
                keys_ref, hi_ref, lo_ref, lo2_ref, jstar_ref, sa_ref, sb_ref, *, k_sel, seq):
    i = pl.program_id(1)
    qb, kc = A_QB, KEY_CHUNK
    n_chunks = (i * qb + qb + kc - 1) // kc
    t_lane = i * qb + lax.broadcasted_iota(jnp.int32, (1, qb), 1)
    sub_pos = lax.broadcasted_iota(jnp.int32, (kc, 1), 0)
    iwt = iwt_ref[...]

    iq = iq_ref[...].reshape(IDX_HEADS * qb, IDX_DIM)

    def index_chunk(c):
        k = ik_ref[pl.ds(pl.multiple_of(c * kc, kc), kc), :]
        rel_all = jnp.maximum(_dot_nt(k, iq), 0.0)
        score = None
        for hd in range(IDX_HEADS):
            rel = rel_all[:, qb * hd:qb * (hd + 1)] * iwt[hd:hd + 1, :]
            score = rel if score is None else score + rel
        bits = lax.bitcast_convert_type(score, jnp.int32)
        key = jnp.where(bits < 0, bits ^ 0x7FFFFFFF, bits)
        key = jnp.where(score == 0.0, 0, key)
        key = jnp.where(c * kc + sub_pos <= t_lane, key, KEY_NEG_INF)
        keys_ref[c] = key
        hi_ref[c] = (key >> 16).astype(jnp.int16)
        lo_ref[c] = ((key & 0xFFFF) - HALF16).astype(jnp.int16)

    def index_pair(pair, carry):
        index_chunk(2 * pair)
        index_chunk(jnp.minimum(2 * pair + 1, n_chunks - 1))
        return carry

    n_pairs = (n_chunks + 1) // 2
    lax.fori_loop(0, n_pairs, index_pair, 0)

    pad_chunk = hi_ref.shape[0] - 1
    hi_ref[pad_chunk] = jnp.full((kc, qb), -HALF16, jnp.int16)
    lo2_ref[pad_chunk] = jnp.full((kc, qb), -HALF16, jnp.int16)
    rows16 = 2 * SUBLANES

    one16, zero16 = jnp.int16(1), jnp.int16(0)

    def paired_sum16(hits):
        def fold(hit):
            parts = [hit[rows16 * r:rows16 * (r + 1), :] for r in range(kc // rows16)]
            while len(parts) > 1:
                parts = [a + b for a, b in zip(parts[0::2], parts[1::2])]
            return parts[0]

        def body(pair, acc):
            second = jnp.where(2 * pair + 1 < n_chunks, 2 * pair + 1, pad_chunk)
            return acc + fold(hits(2 * pair, 2 * pair)) + fold(hits(second, 2 * pair + 1))
        acc = lax.fori_loop(0, n_pairs, body, jnp.zeros((rows16, qb), jnp.int16))
        return jnp.sum(acc.astype(jnp.int32), axis=0, keepdims=True)

    def count16(ref, pred):
        return paired_sum16(lambda c, _: jnp.where(pred(ref[c]), one16, zero16))

    def kth_largest16(ref, k_row):
        u = jnp.where(count16(ref, lambda x: x >= jnp.int16(0)) >= k_row, 0, -HALF16)

        def bit(it, u):
            cand = u + jnp.left_shift(jnp.int32(1), 14 - it)
            c16 = cand.astype(jnp.int16)
            return jnp.where(count16(ref, lambda x: x >= c16) >= k_row, cand, u)
        return lax.fori_loop(0, 15, bit, u)

    v_hi = kth_largest16(hi_ref, k_sel)
    v_hi16 = v_hi.astype(jnp.int16)
    k_low = k_sel - count16(hi_ref, lambda x: x > v_hi16)

    def mask_low(c, carry):
        lo2_ref[c] = jnp.where(hi_ref[c] == v_hi16, lo_ref[c], jnp.int16(-HALF16))
        return carry

    lax.fori_loop(0, n_chunks, mask_low, 0)
    v_lo = kth_largest16(lo2_ref, k_low)
    v_lo16 = v_lo.astype(jnp.int16)
    v = v_hi * (2 * HALF16) + (v_lo + HALF16)
    count_ge = (k_sel - k_low) + count16(lo2_ref, lambda x: x >= v_lo16)

    jstar_ref[...] = jnp.full(jstar_ref.shape, seq - 1, jnp.int32)

    @pl.when(jnp.max(count_ge) > k_sel)
    def _():
        need = k_low - count16(lo2_ref, lambda x: x > v_lo16)

        def mark_tied(c, carry):
            lo2_ref[c] = jnp.where(hi_ref[c] == v_hi16, jnp.where(lo_ref[c] == v_lo16, one16, zero16), zero16)
            return carry

        lax.fori_loop(0, n_chunks, mark_tied, 0)
        lo2_ref[pad_chunk] = jnp.zeros((kc, qb), jnp.int16)
        pos16 = lax.broadcasted_iota(jnp.int32, (kc, qb), 0).astype(jnp.int16)

        def tied_below(cand):
            def hits(c, c_pos):
                local = jnp.clip(cand - c_pos * kc, 0, kc).astype(jnp.int16)
                return jnp.where(pos16 < local, lo2_ref[c], zero16)
            return paired_sum16(hits)

        def index_bit(it, x):
            cand = x + jnp.left_shift(jnp.int32(1), (seq.bit_length() - 2) - it)
            return jnp.where(tied_below(cand) < need, cand, x)

        x = lax.fori_loop(0, seq.bit_length() - 1, index_bit, jnp.zeros((1, qb), jnp.int32))
        jstar_ref[...] = jnp.broadcast_to(x, jstar_ref.shape)

    j_star = jstar_ref[0:1, :]

    q = qc_ref[...].reshape(A_HEADS * qb, 2 * LANES)

    def scores(c):
        cc = jnp.minimum(c, n_chunks - 1)
        kv = kc_ref[pl.ds(pl.multiple_of(cc * kc, kc), kc), :]
        pos = c * kc + sub_pos
        key = keys_ref[cc]
        key = jnp.where(pos > j_star, key - 1, key)
        bias = jnp.where(key >= v, jnp.where(pos <= t_lane, 0.0, MASKED), MASKED)
        return _dot_nt(kv, q) + jnp.concatenate([bias] * A_HEADS, axis=1)

    groups = (A_HEADS * qb) // ATTEND_LANES

    def update(s_ref, c, states):
        cc = jnp.minimum(c, n_chunks - 1)
        vt = cvt_ref[:, pl.ds(pl.multiple_of(cc * kc, kc), kc)]
        return tuple(_flash_update(s_ref[:, ATTEND_LANES * g:ATTEND_LANES * (g + 1)], vt, states[g]) for g in range(groups))

    sa_ref[...] = scores(0)

    def attend(pair, state):
        sb_ref[...] = scores(2 * pair + 1)
        state = update(sa_ref, 2 * pair, state)
        sa_ref[...] = scores(2 * pair + 2)
        return update(sb_ref, 2 * pair + 1, state)

    states = lax.fori_loop(0, n_pairs, attend, tuple(_flash_init(A_KV_RANK, ATTEND_LANES) for _ in range(groups)))
    o = jnp.concatenate([_flash_result(st, A_KV_RANK) for st in states], axis=1)
    for hd in range(A_HEADS):
        o_ref[:, A_KV_RANK * hd:A_KV_RANK * (hd + 1)] = o[:, qb * hd:qb * (hd + 1)].T.astype(BF16)


def _dsa(iq, ik, iwt, qc, kc, cvt, batch, seq):
    n = ik.shape[0]
    nq = seq // A_QB
    k_sel = min(TOPK_MAX, seq // 4)
    assert seq & (seq - 1) == 0 and seq % KEY_CHUNK == 0 and k_sel <= KEY_CHUNK and A_QB % LANES == 0
    qrow = lambda b, i: (b * nq + i, 0)
    hqrow = lambda b, i: (0, b * nq + i, 0)
    brow = lambda b, i: (b, 0)
    return pl.pallas_call(
        functools.partial(_dsa_kernel, k_sel=k_sel, seq=seq),
        grid=(batch, nq),
        in_specs=[
            pl.BlockSpec((IDX_HEADS, A_QB, IDX_DIM), hqrow),
            pl.BlockSpec((seq, IDX_DIM), brow),
            pl.BlockSpec((IDX_HEADS, A_QB), lambda b, i: (0, b * nq + i)),
            pl.BlockSpec((A_HEADS, A_QB, 2 * LANES), hqrow),
            pl.BlockSpec((seq, 2 * LANES), brow),
            pl.BlockSpec((A_KV_RANK, seq), lambda b, i: (0, b)),
        ],
        out_specs=pl.BlockSpec((A_QB, A_HEADS * A_KV_RANK), qrow),
        out_shape=jax.ShapeDtypeStruct((n, A_HEADS * A_KV_RANK), BF16),
        scratch_shapes=[pltpu.VMEM((seq // KEY_CHUNK, KEY_CHUNK, A_QB), jnp.int32),
                        pltpu.VMEM((seq // KEY_CHUNK + 1, KEY_CHUNK, A_QB), jnp.int16),
                        pltpu.VMEM((seq // KEY_CHUNK, KEY_CHUNK, A_QB), jnp.int16),
                        pltpu.VMEM((seq // KEY_CHUNK + 1, KEY_CHUNK, A_QB), jnp.int16),
                        pltpu.VMEM((SUBLANES, A_QB), jnp.int32),
                        pltpu.VMEM((KEY_CHUNK, A_HEADS * A_QB), F32),
                        pltpu.VMEM((KEY_CHUNK, A_HEADS * A_QB), F32)],
        compiler_params=_cparams(("parallel", "arbitrary")),
        name="dsa",
    )(iq, ik, iwt, qc, kc, cvt)


def _diff_kernel(q_ref, k_ref, vt_ref, lam_ref, g_ref, o_ref, sa_ref, sb_ref, *, lam_init):
    i = pl.program_id(1)
    qb, kc = B_QB, KEY_CHUNK
    nsub = 2 * B_HEADS
    causal = (lax.broadcasted_iota(jnp.int32, (kc, qb), 0) <= lax.broadcasted_iota(jnp.int32, (kc, qb), 1))
    tri = jnp.where(causal, 0.0, MASKED)

    def scores(c):
        start = pl.multiple_of(jnp.minimum(c, i) * kc, kc)
        bias = jnp.where(c < i, 0.0, jnp.where(c == i, tri, MASKED))
        return jnp.concatenate([_dot_nt(k_ref[j, pl.ds(start, kc), :], q_ref[j]) + bias for j in range(nsub)], axis=1)

    def update(s_ref, c, states):
        start = pl.multiple_of(jnp.minimum(c, i) * kc, kc)
        return tuple(_flash_update(s_ref[:, 2 * qb * hd:2 * qb * (hd + 1)],
                                   vt_ref[B_V_DIM * hd:B_V_DIM * (hd + 1), pl.ds(start, kc)], states[hd])
                     for hd in range(B_HEADS))

    sa_ref[...] = scores(0)

    def sweep_pair(pair, states):
        sb_ref[...] = scores(2 * pair + 1)
        states = update(sa_ref, 2 * pair, states)
        sa_ref[...] = scores(2 * pair + 2)
        return update(sb_ref, 2 * pair + 1, states)

    states = lax.fori_loop(0, (i + 2) // 2, sweep_pair, tuple(_flash_init(B_V_DIM, 2 * qb) for _ in range(B_HEADS)))

    lam = lam_ref[...]
    lam_full = (jnp.exp(jnp.sum(lam[0:1] * lam[1:2], axis=-1, keepdims=True))
                - jnp.exp(jnp.sum(lam[2:3] * lam[3:4], axis=-1, keepdims=True)) + lam_init)
    gain_col = g_ref[...]
    for hd in range(B_HEADS):
        o = _flash_result(states[hd], B_V_DIM)
        d = o[:, 0:qb] - lam_full * o[:, qb:2 * qb]
        d = d * lax.rsqrt(jnp.mean(d * d, axis=0, keepdims=True) + NORM_EPS) * gain_col * (1.0 - lam_init)
        o_ref[:, B_V_DIM * hd:B_V_DIM * (hd + 1)] = d.T.astype(BF16)


def _diff(qb, kb, vbt, lam, subln, lam_init, batch, seq):
    n = vbt.shape[1]
    nq = seq // B_QB
    assert B_QB == KEY_CHUNK
    return pl.pallas_call(
        functools.partial(_diff_kernel, lam_init=lam_init),
        grid=(batch, nq),
        in_specs=[
            pl.BlockSpec((2 * B_HEADS, B_QB, B_QK_DIM), lambda b, i: (0, b * nq + i, 0)),
            pl.BlockSpec((2 * B_HEADS, seq, B_QK_DIM), lambda b, i: (0, b, 0)),
            pl.BlockSpec((B_HEADS * B_V_DIM, seq), lambda b, i: (0, b)),
            pl.BlockSpec((4, B_QK_DIM), lambda b, i: (0, 0)),
            pl.BlockSpec((B_V_DIM, 1), lambda b, i: (0, 0)),
        ],
        out_specs=pl.BlockSpec((B_QB, B_HEADS * B_V_DIM), lambda b, i: (b * nq + i, 0)),
        out_shape=jax.ShapeDtypeStruct((n, B_HEADS * B_V_DIM), BF16),
        scratch_shapes=[pltpu.VMEM((KEY_CHUNK, 2 * B_HEADS * B_QB), F32), pltpu.VMEM((KEY_CHUNK, 2 * B_HEADS * B_QB), F32)],
        compiler_params=_cparams(("parallel", "arbitrary")),
        name="diff_attn",
    )(qb, kb, vbt, lam, subln)


def _even_out_kernel(x_ref, ol_ref, ob_ref, wuv_ref, wo_ref, y_ref):
    half = A_HEADS * A_V_DIM
    o_a = _dot(ol_ref[...], wuv_ref[...]).astype(BF16)
    y_ref[...] = x_ref[...] + _dot(o_a, wo_ref[0:half, :]) + _dot(ob_ref[...], wo_ref[half:, :])


def _even_out(x2, o_lat, o_b, wuv_bd, w_out):
    n = x2.shape[0]
    tm = PROJ_TM
    row = lambda i: (i, 0)
    full2 = lambda i: (0, 0)
    return pl.pallas_call(
        _even_out_kernel,
        grid=(n // tm,),
        in_specs=[
            pl.BlockSpec((tm, D_MODEL), row),
            pl.BlockSpec((tm, A_HEADS * A_KV_RANK), row),
            pl.BlockSpec((tm, B_HEADS * B_V_DIM), row),
            pl.BlockSpec((A_HEADS * A_KV_RANK, A_HEADS * A_V_DIM), full2),
            pl.BlockSpec((D_MODEL, D_MODEL), full2),
        ],
        out_specs=pl.BlockSpec((tm, D_MODEL), row),
        out_shape=jax.ShapeDtypeStruct((n, D_MODEL), F32),
        compiler_params=_cparams(("parallel",)),
        name="even_out",
    )(x2, o_lat, o_b, wuv_bd, w_out)


OD_R64 = (0, 1792)
OD_V = (1792, 2560)
OD_G = (2560, 3072)
OD_COLS = 3072
GATES_PER_GROUP = C_HPG * 3
KV_WIDTH = C_GROUPS * C_DIM


def _odd_weight(w_in):
    qc, kc, vc, ks, vs, kw, vw, gc = jnp.split(w_in, np.cumsum(ODD_SPLITS)[:-1].tolist(), axis=1)
    d = w_in.shape[0]
    gates = jnp.pad(gc.reshape(d, C_GROUPS, GATES_PER_GROUP), ((0, 0), (0, 0), (0, LANES - GATES_PER_GROUP)))
    w = jnp.concatenate([qc, kc, ks, kw, vc, vs, vw, gates.reshape(d, C_GROUPS * LANES)], axis=1)
    assert w.shape[1] == OD_COLS
    return w.astype(BF16)


def _odd_proj_kernel(x_ref, g_ref, w_ref, t64_ref, q_ref, kc_ref, ks_ref, kw_ref, vc_ref, vst_ref, vwt_ref, gt_ref):
    h = _rms_rows(x_ref[...], g_ref[...]).astype(BF16)
    half = LANES // 2
    cos, s_hi, s_lo = t64_ref[0], t64_ref[1], t64_ref[2]
    y = _dot(h, w_ref[:, OD_R64[0]:OD_R64[1]])
    for blk in range((OD_R64[1] - OD_R64[0]) // LANES):
        r = _rope_block(y[:, LANES * blk:LANES * (blk + 1)], cos, s_hi, s_lo, C_DIM // 2)
        if blk < 8:
            r = (r * (C_DIM ** -0.5 * LOG2E)).astype(BF16)
            dst, j = q_ref, blk
        else:
            r = r.astype(BF16)
            dst, j = (kc_ref, ks_ref, kw_ref)[(blk - 8) // 2], (blk - 8) % 2
        dst[2 * j] = r[:, :half]
        dst[2 * j + 1] = r[:, half:]
    y = _dot(h, w_ref[:, OD_V[0]:OD_V[1]])
    vc = y[:, 0:KV_WIDTH].astype(BF16)
    for g in range(C_GROUPS):
        vc_ref[g] = vc[:, C_DIM * g:C_DIM * (g + 1)]
    vst_ref[...] = y[:, KV_WIDTH:2 * KV_WIDTH].T.astype(BF16)
    vwt_ref[...] = y[:, 2 * KV_WIDTH:3 * KV_WIDTH].T.astype(BF16)
    y = jax.nn.sigmoid(_dot(h, w_ref[:, OD_G[0]:OD_G[1]]))
    for g in range(C_GROUPS):
        gt_ref[g] = y[:, LANES * g:LANES * (g + 1)]


def _odd_proj(x2, gain, w, t64, seq):
    n = x2.shape[0]
    tm = PROJ_TM
    tiles_per_seq = seq // tm
    row = lambda i: (i, 0)
    col = lambda i: (0, i)
    hrow = lambda i: (0, i, 0)
    full2 = lambda i: (0, 0)
    kv_shape = jax.ShapeDtypeStruct((C_GROUPS, n, C_DIM), BF16)
    kv_spec = pl.BlockSpec((C_GROUPS, tm, C_DIM), hrow)
    kvt_shape = jax.ShapeDtypeStruct((KV_WIDTH, n), BF16)
    kvt_spec = pl.BlockSpec((KV_WIDTH, tm), col)
    return pl.pallas_call(
        _odd_proj_kernel,
        grid=(n // tm,),
        in_specs=[
            pl.BlockSpec((tm, D_MODEL), row),
            pl.BlockSpec((1, D_MODEL), full2),
            pl.BlockSpec((D_MODEL, OD_COLS), full2),
            pl.BlockSpec((3, tm, LANES), lambda i: (0, i % tiles_per_seq, 0)),
        ],
        out_specs=(pl.BlockSpec((C_HEADS, tm, C_DIM), hrow),) + (kv_spec,) * 4 + (kvt_spec,) * 2
        + (pl.BlockSpec((C_GROUPS, tm, LANES), hrow),),
        out_shape=(jax.ShapeDtypeStruct((C_HEADS, n, C_DIM), BF16),) + (kv_shape,) * 4 + (kvt_shape,) * 2
        + (jax.ShapeDtypeStruct((C_GROUPS, n, LANES), F32),),
        compiler_params=_cparams(("parallel",)),
        name="odd_proj",
    )(x2, gain, w, t64)


def _gelu_tanh(x):
    return 0.5 * x * (1.0 + jnp.tanh(math.sqrt(2.0 / math.pi) * (x + 0.044715 * (x * x * x))))


def _compress_kernel(kch_ref, vch_ref, pe_ref, w1_ref, w2_ref, w2t_ref, kc_ref, vct_ref):
    rows = kch_ref.shape[2]

    def hidden(src, kv):
        ch = src[0, 0].astype(F32)
        first = _dot((ch + pe_ref[kv, 0:1, :]).astype(BF16), w1_ref[kv, 0])
        second = _dot((ch + pe_ref[kv, 1:2, :]).astype(BF16), w1_ref[kv, 1])
        return _gelu_tanh(first + pltpu.roll(second, rows - 1, 0)).astype(BF16)

    kc_ref[0, 0] = _dot(hidden(kch_ref, 0), w2_ref[...]).astype(BF16)
    vct_ref[0, 0] = _dot_nt(w2t_ref[...], hidden(vch_ref, 1)).astype(BF16)


def _compress(kc_raw, vc_raw, pe, w1, w2, batch, seq):
    nchunk = seq // CMP_STRIDE
    width = CMP_STRIDE * C_DIM
    kch = kc_raw.reshape(C_GROUPS, batch, nchunk, width)
    vch = vc_raw.reshape(C_GROUPS, batch, nchunk, width)
    pe2 = pe.reshape(2, 2, width)
    w1s = w1.reshape(2, 2, width, CMP_HIDDEN).astype(BF16)
    blk = lambda g, b: (g, b, 0, 0)
    full2 = lambda g, b: (0, 0)
    return pl.pallas_call(
        _compress_kernel,
        grid=(C_GROUPS, batch),
        in_specs=[
            pl.BlockSpec((1, 1, nchunk, width), blk),
            pl.BlockSpec((1, 1, nchunk, width), blk),
            pl.BlockSpec((2, 2, width), lambda g, b: (0, 0, 0)),
            pl.BlockSpec((2, 2, width, CMP_HIDDEN), lambda g, b: (0, 0, 0, 0)),
            pl.BlockSpec((CMP_HIDDEN, C_DIM), full2),
            pl.BlockSpec((C_DIM, CMP_HIDDEN), full2),
        ],
        out_specs=(pl.BlockSpec((1, 1, nchunk, C_DIM), blk), pl.BlockSpec((1, 1, C_DIM, nchunk), blk)),
        out_shape=(jax.ShapeDtypeStruct((C_GROUPS, batch, nchunk, C_DIM), BF16),
                   jax.ShapeDtypeStruct((C_GROUPS, batch, C_DIM, nchunk), BF16)),
        compiler_params=_cparams(("parallel", "parallel")),
        name="nsa_compress",
    )(kch, vch, pe2, w1s, w2[0].astype(BF16), w2[1].T.astype(BF16))


def _split3(x):
    a = x.astype(BF16)
    r = x - a.astype(F32)
    b = r.astype(BF16)
    c = (r - b.astype(F32)).astype(BF16)
    return a, b, c


def _nsa_kernel(q_ref, kc_ref, vct_ref, ks_ref, vst_ref, kw_ref, vwt_ref, gt_ref, o_ref, sa_ref, sb_ref, sc_ref,
                *, n_blk, top_n):
    i = pl.program_id(2)
    qb, kc = C_QB, KEY_CHUNK
    q = q_ref[...].reshape(C_HPG * qb, C_DIM)
    heads = lambda a: jnp.concatenate([a] * C_HPG, axis=1)
    t_lane = i * qb + lax.broadcasted_iota(jnp.int32, (1, qb), 1)
    sub_pos = lax.broadcasted_iota(jnp.int32, (kc, 1), 0)

    n_cmp = kc_ref.shape[2]
    cmp_end = lax.broadcasted_iota(jnp.int32, (n_cmp, 1), 0) * CMP_STRIDE + (CMP_LEN - 1)
    vis = heads(jnp.where(cmp_end <= t_lane, 1.0, 0.0))
    s = _dot_nt(kc_ref[0, 0], q) + (vis - 1.0) * (-MASKED)
    e = jnp.exp2(s - jnp.max(s, axis=0, keepdims=True)) * vis
    den = jnp.sum(e, axis=0, keepdims=True)
    p_cmp = e / jnp.where(den > 0, den, 1.0)
    o_cmp = _dot(vct_ref[0, 0], p_cmp.astype(BF16))

    p_sum = p_cmp[:, 0:qb]
    for hd in range(1, C_HPG):
        p_sum = p_sum + p_cmp[:, qb * hd:qb * (hd + 1)]
    blk_id = lax.broadcasted_iota(jnp.int32, (n_blk, n_cmp), 0)
    cmp_id = lax.broadcasted_iota(jnp.int32, (n_blk, n_cmp), 1)
    ratio = SEL_BLOCK // CMP_STRIDE
    overlap_t = ((cmp_id < ratio * (blk_id + 1)) & (cmp_id + CMP_LEN // CMP_STRIDE > ratio * blk_id))
    overlap_t = jnp.where(overlap_t, 1.0, 0.0).astype(BF16)
    imp = sum(_dot(overlap_t, part) for part in _split3(p_sum))
    j_sub = lax.broadcasted_iota(jnp.int32, (n_blk, 1), 0)
    cur = t_lane // SEL_BLOCK
    forced = (j_sub == 0) | (j_sub == cur) | (j_sub == cur - 1)
    imp = jnp.where(forced, jnp.inf, imp)
    imp = jnp.where(j_sub * SEL_BLOCK <= t_lane, imp, -jnp.inf)
    ranks = []
    for tile in range(n_blk // SUBLANES):
        mine = imp[SUBLANES * tile:SUBLANES * (tile + 1), :]
        j_tile = j_sub[SUBLANES * tile:SUBLANES * (tile + 1), :]
        rank = jnp.zeros((SUBLANES, qb), jnp.int32)
        for other in range(n_blk):
            row = imp[other:other + 1, :]
            ge = jnp.where(row >= mine, 1, 0)
            gt = jnp.where(row > mine, 1, 0)
            if other < SUBLANES * tile:
                rank = rank + ge
            elif other >= SUBLANES * (tile + 1):
                rank = rank + gt
            else:
                rank = rank + jnp.where(j_tile > other, ge, gt)
        ranks.append(rank)
    sel_bias = jnp.where(jnp.concatenate(ranks, axis=0) < top_n, 0.0, MASKED).astype(BF16)

    init = _flash_init(C_DIM, C_HPG * qb)

    key_blk = lax.broadcasted_iota(jnp.int32, (kc, n_blk), 0) // SEL_BLOCK
    blk_col = lax.broadcasted_iota(jnp.int32, (kc, n_blk), 1)

    def slc_scores(c):
        cc = jnp.minimum(c, i)
        expand = jnp.where(blk_col == key_blk + cc * (kc // SEL_BLOCK), 1.0, 0.0).astype(BF16)
        block_bias = _dot(expand, sel_bias)
        bias = jnp.where(c * kc + sub_pos <= t_lane, block_bias, MASKED)
        k = ks_ref[0, pl.ds(pl.multiple_of(cc * kc, kc), kc), :]
        return _dot_nt(k, q) + heads(bias)

    def slc_update(s_ref, c, state):
        cc = jnp.minimum(c, i)
        return _flash_update(s_ref[...], vst_ref[:, pl.ds(pl.multiple_of(cc * kc, kc), kc)], state)

    sa_ref[...] = slc_scores(0)

    def slc_pair(pair, state):
        sb_ref[...] = slc_scores(2 * pair + 1)
        state = slc_update(sa_ref, 2 * pair, state)
        sa_ref[...] = slc_scores(2 * pair + 2)
        return slc_update(sb_ref, 2 * pair + 1, state)

    o_slc = _flash_result(lax.fori_loop(0, (i + 2) // 2, slc_pair, init), C_DIM)

    assert qb == kc and WINDOW % kc == 0
    back = WINDOW // kc
    assert back + 1 == 3
    key_i = lax.broadcasted_iota(jnp.int32, (kc, qb), 0)
    lane_i = lax.broadcasted_iota(jnp.int32, (kc, qb), 1)
    stage = (sa_ref, sb_ref, sc_ref)
    for step in range(back + 1):
        c = i - back + step
        exists = jnp.where(c >= 0, 0.0, MASKED)
        if step == 0:
            bias = jnp.where(key_i > lane_i, exists, MASKED)
        elif step < back:
            bias = jnp.where(key_i >= 0, exists, MASKED)
        else:
            bias = jnp.where(key_i <= lane_i, 0.0, MASKED)
        start = pl.multiple_of(jnp.maximum(c, 0) * kc, kc)
        stage[step][...] = _dot_nt(kw_ref[0, pl.ds(start, kc), :], q) + heads(bias)
    win = init
    for step in range(back + 1):
        start = pl.multiple_of(jnp.maximum(i - back + step, 0) * kc, kc)
        win = _flash_update(stage[step][...], vwt_ref[:, pl.ds(start, kc)], win)
    o_win = _flash_result(win, C_DIM)

    gates_t = gt_ref[0].T
    cols = []
    for hd in range(C_HPG):
        g0, g1, g2 = (gates_t[3 * hd + j:3 * hd + j + 1, :] for j in range(3))
        blk = slice(qb * hd, qb * (hd + 1))
        cols.append(g0 * o_cmp[:, blk] + g1 * o_slc[:, blk] + g2 * o_win[:, blk])
    o_ref[...] = jnp.concatenate(cols, axis=0).T.astype(BF16)


def _nsa(q, k_cmp, v_cmp_t, ks, vst, kw, vwt, gates, batch, seq):
    n = ks.shape[1]
    nq = seq // C_QB
    n_blk = seq // SEL_BLOCK
    top_n = min(SEL_TOPN, n_blk)
    nchunk = seq // CMP_STRIDE
    k_spec = pl.BlockSpec((1, seq, C_DIM), lambda b, g, i: (g, b, 0))
    vt_spec = pl.BlockSpec((C_DIM, seq), lambda b, g, i: (g, b))
    return pl.pallas_call(
        functools.partial(_nsa_kernel, n_blk=n_blk, top_n=top_n),
        grid=(batch, C_GROUPS, nq),
        in_specs=[
            pl.BlockSpec((C_HPG, C_QB, C_DIM), lambda b, g, i: (g, b * nq + i, 0)),
            pl.BlockSpec((1, 1, nchunk, C_DIM), lambda b, g, i: (g, b, 0, 0)),
            pl.BlockSpec((1, 1, C_DIM, nchunk), lambda b, g, i: (g, b, 0, 0)),
            k_spec, vt_spec, k_spec, vt_spec,
            pl.BlockSpec((1, C_QB, LANES), lambda b, g, i: (g, b * nq + i, 0)),
        ],
        out_specs=pl.BlockSpec((C_QB, C_HPG * C_DIM), lambda b, g, i: (b * nq + i, g)),
        out_shape=jax.ShapeDtypeStruct((n, C_HEADS * C_DIM), BF16),
        scratch_shapes=[pltpu.VMEM((KEY_CHUNK, C_HPG * C_QB), F32)] * 3,
        compiler_params=_cparams(("parallel", "parallel", "arbitrary")),
        name="nsa",
    )(q, k_cmp, v_cmp_t, ks, vst, kw, vwt, gates)


def _odd_out_kernel(x_ref, o_ref, wo_ref, y_ref):
    y_ref[...] = x_ref[...] + _dot(o_ref[...], wo_ref[...])


def _odd_out(x2, o_c, w_out):
    n = x2.shape[0]
    tm = PROJ_TM
    row = lambda i: (i, 0)
    return pl.pallas_call(
        _odd_out_kernel,
        grid=(n // tm,),
        in_specs=[
            pl.BlockSpec((tm, D_MODEL), row),
            pl.BlockSpec((tm, D_MODEL), row),
            pl.BlockSpec((D_MODEL, D_MODEL), lambda i: (0, 0)),
        ],
        out_specs=pl.BlockSpec((tm, D_MODEL), row),
        out_shape=jax.ShapeDtypeStruct((n, D_MODEL), F32),
        compiler_params=_cparams(("parallel",)),
        name="odd_out",
    )(x2, o_c, w_out)


FFN_CHUNKS = 2
FFN_SUBCHUNKS = 1


def _ffn_kernel(x_ref, g_ref, wg_ref, wu_ref, wd_ref, gf_ref, y_ref, h_ref, acc_ref, *, final_norm):
    f = pl.program_id(1)

    @pl.when(f == 0)
    def _():
        h_ref[...] = _rms_rows(x_ref[...], g_ref[...]).astype(BF16)

    h = h_ref[...]
    part = None
    sub = wg_ref.shape[1] // FFN_SUBCHUNKS
    for j in range(FFN_SUBCHUNKS):
        gate = _dot(h, wg_ref[:, sub * j:sub * (j + 1)])
        act = (gate * jax.nn.sigmoid(gate) * _dot(h, wu_ref[:, sub * j:sub * (j + 1)])).astype(BF16)
        down = _dot(act, wd_ref[sub * j:sub * (j + 1), :])
        part = down if part is None else part + down

    @pl.when(f == 0)
    def _():
        acc_ref[...] = part

    @pl.when(f == FFN_CHUNKS - 1)
    def _():
        y = x_ref[...] + acc_ref[...] + part if FFN_CHUNKS > 1 else x_ref[...] + part
        y_ref[...] = _rms_rows(y, gf_ref[...]) if final_norm else y


def _ffn(x2, gain, wg, wu, wd, g_final, final_norm):
    n = x2.shape[0]
    tm = PROJ_TM
    fc = D_FF // FFN_CHUNKS
    assert FFN_CHUNKS in (1, 2) and fc % LANES == 0
    row = lambda i, f: (i, 0)
    full2 = lambda i, f: (0, 0)
    return pl.pallas_call(
        functools.partial(_ffn_kernel, final_norm=final_norm),
        grid=(n // tm, FFN_CHUNKS),
        in_specs=[
            pl.BlockSpec((tm, D_MODEL), row),
            pl.BlockSpec((1, D_MODEL), full2),
            pl.BlockSpec((D_MODEL, fc), lambda i, f: (0, f)),
            pl.BlockSpec((D_MODEL, fc), lambda i, f: (0, f)),
            pl.BlockSpec((fc, D_MODEL), lambda i, f: (f, 0)),
            pl.BlockSpec((1, D_MODEL), full2),
        ],
        out_specs=pl.BlockSpec((tm, D_MODEL), row),
        out_shape=jax.ShapeDtypeStruct((n, D_MODEL), F32),
        scratch_shapes=[pltpu.VMEM((tm, D_MODEL), BF16), pltpu.VMEM((tm, D_MODEL), F32)],
        compiler_params=_cparams(("parallel", "arbitrary")),
        name="ffn",
    )(x2, gain, wg, wu, wd, g_final)


def kernel(x, norm_mix, norm_ffn, norm_final, ev_w_in, ev_kv_gain, ev_w_uk, ev_w_uv, ev_lambda, ev_subln, ev_w_out,
           od_w_in, od_cmp_pe, od_cmp_w1, od_cmp_w2, od_w_out, ffn_w_gate, ffn_w_up, ffn_w_down):
    batch, seq, d = x.shape
    depth = norm_mix.shape[0]
    x2 = x.reshape(batch * seq, d)
    t64 = _rope_tables(seq, 64)
    t32 = _rope_tables(seq, A_ROPE)
    g_final = norm_final.reshape(1, d)
    for layer in range(depth):
        j = layer // 2
        gain = norm_mix[layer].reshape(1, d)
        if layer % 2 == 0:
            qc, kc, cvt, iq, ik, iwt, qb, kb, vbt = _even_proj(
                x2, gain, _even_weight(ev_w_in[j]), _block_diag(ev_w_uk[j]).astype(BF16),
                ev_kv_gain[j].reshape(1, A_KV_RANK), t64, t32, seq)
            o_lat = _dsa(iq, ik, iwt, qc, kc, cvt, batch, seq)
            lam_init = 0.8 - 0.6 * math.exp(-0.3 * layer)
            o_b = _diff(qb, kb, vbt, ev_lambda[j], ev_subln[j].reshape(B_V_DIM, 1), lam_init, batch, seq)
            x2 = _even_out(x2, o_lat, o_b, _block_diag(ev_w_uv[j]).astype(BF16), ev_w_out[j].astype(BF16))
        else:
            q, kc_raw, ks, kw, vc_raw, vst, vwt, gates = _odd_proj(x2, gain, _odd_weight(od_w_in[j]), t64, seq)
            k_cmp, v_cmp_t = _compress(kc_raw, vc_raw, od_cmp_pe[j], od_cmp_w1[j], od_cmp_w2[j], batch, seq)
            o_c = _nsa(q, k_cmp, v_cmp_t, ks, vst, kw, vwt, gates, batch, seq)
            x2 = _odd_out(x2, o_c, od_w_out[j].astype(BF16))
        x2 = _ffn(x2, norm_ffn[layer].reshape(1, d), ffn_w_gate[layer].astype(BF16), ffn_w_up[layer].astype(BF16),
                  ffn_w_down[layer].astype(BF16), g_final, layer == depth - 1)
    return x2.reshape(batch, seq, d)
```

```python
import functools
import math

import numpy as np
import jax
import jax.numpy as jnp
from jax import lax
from jax.experimental import pallas as pl
from jax.experimental.pallas import tpu as pltpu

F32 = jnp.float32
BF16 = jnp.bfloat16

D_MODEL = 1024
ROPE_THETA = 10000.0
NORM_EPS = 1e-6

A_HEADS = 8
A_NOPE = 64
A_ROPE = 32
A_KV_RANK = 128
A_V_DIM = 64
A_SCALE = (A_NOPE + A_ROPE) ** -0.5
IDX_HEADS = 8
IDX_DIM = 64
TOPK_MAX = 256

B_HEADS = 4
B_QK_DIM = 64
B_V_DIM = 2 * B_QK_DIM

C_HEADS = 16
C_GROUPS = 4
C_HPG = C_HEADS // C_GROUPS
C_DIM = 64
CMP_LEN = 32
CMP_STRIDE = 16
CMP_HIDDEN = 128
SEL_BLOCK = 64
SEL_TOPN = 16
WINDOW = 512

D_FF = -(-8 * D_MODEL // (3 * 256)) * 256

EVEN_SPLITS = [A_HEADS * A_NOPE, A_HEADS * A_ROPE, A_KV_RANK, A_ROPE, IDX_HEADS * IDX_DIM, IDX_DIM, IDX_HEADS,
               B_HEADS * 2 * B_QK_DIM, B_HEADS * 2 * B_QK_DIM, B_HEADS * B_V_DIM]
ODD_SPLITS = [C_HEADS * C_DIM] + [C_GROUPS * C_DIM] * 6 + [C_HEADS * 3]

LANES = 128
SUBLANES = 8
VMEM_LIMIT = 56 * 1024 * 1024
MASKED = -1e30
KEY_NEG_INF = -2139095041
HALF16 = 1 << 15
LOG2E = math.log2(math.e)

PROJ_TM = 512
A_QB = 256
KEY_CHUNK = 256
B_QB = 512
C_QB = 256


def _cparams(sem):
    return pltpu.CompilerParams(dimension_semantics=sem, vmem_limit_bytes=VMEM_LIMIT)


def _dot(a, b):
    return jnp.dot(a, b, preferred_element_type=F32)


def _dot_nt(a, b):
    return lax.dot_general(a, b, (((1,), (1,)), ((), ())), preferred_element_type=F32)


def _rms_rows(x, gain):
    return x * lax.rsqrt(jnp.mean(x * x, axis=-1, keepdims=True) + NORM_EPS) * gain


def _rope_block(y, cos, s_hi, s_lo, half):
    return y * cos + pltpu.roll(y, half, 1) * s_hi + pltpu.roll(y, LANES - half, 1) * s_lo


def _rope_tables(seq, d):
    half = d // 2
    pos = jnp.arange(seq, dtype=jnp.int32)
    inv = ROPE_THETA ** (-jnp.arange(0, d, 2, dtype=F32) / d)
    ang = pos.astype(F32)[:, None] * inv[None, :]
    cos, sin = jnp.cos(ang), jnp.sin(ang)
    lane = np.arange(LANES)
    idx = (lane % d) % half
    first = jnp.asarray((lane % d) < half)
    c = cos[:, idx]
    s = sin[:, idx]
    return jnp.stack([c, jnp.where(first[None, :], 0.0, s), jnp.where(first[None, :], -s, 0.0)])


ONES_ROWS = 16


def _flash_probs(s, m):
    m_new = jnp.maximum(m, jnp.max(s, axis=0, keepdims=True))
    return m_new, jnp.exp2(m - m_new), jnp.exp2(s - m_new).astype(BF16)


def _flash_accumulate(acc, alpha, vt, p):
    vt_ones = jnp.concatenate([vt, jnp.ones((ONES_ROWS, vt.shape[1]), BF16)], axis=0)
    return alpha * acc + _dot(vt_ones, p)


def _flash_update(s, vt, state):
    m, acc = state
    m_new, alpha, p = _flash_probs(s, m)
    return m_new, _flash_accumulate(acc, alpha, vt, p)


def _attend_pairs(n_pairs, scores, values, sa_ref, sb_ref, dv, group_lanes):
    groups = sa_ref.shape[1] // group_lanes

    def update(s_ref, c, states):
        vts = values(c)
        return tuple(_flash_update(s_ref[:, group_lanes * g:group_lanes * (g + 1)], vts[g], states[g])
                     for g in range(groups))

    sa_ref[...] = scores(0)

    def trip(pair, states):
        sb_ref[...] = scores(2 * pair + 1)
        states = update(sa_ref, 2 * pair, states)
        sa_ref[...] = scores(2 * pair + 2)
        return update(sb_ref, 2 * pair + 1, states)

    states = lax.fori_loop(0, n_pairs, trip, tuple(_flash_init(dv, group_lanes) for _ in range(groups)))
    return tuple(_flash_result(st, dv) for st in states)


def _flash_init(dv, lanes):
    return (jnp.full((1, lanes), MASKED, F32), jnp.zeros((dv + ONES_ROWS, lanes), F32))


def _flash_result(state, dv):
    _, acc = state
    return acc[0:dv] / acc[dv:dv + 1]


EV_QN = (0, 512)
EV_R64 = (512, 2176)
EV_R32 = (2176, 3328)
EV_CKV = (3328, 3456)
EV_VB = (3456, 3968)
EV_IW = (3968, 4096)
EV_COLS = 4096


def _even_weight(w_in):
    qa_nope, qa_rope, c_kv, ka_rope, iq, ik, iw, qb, kb, vb = jnp.split(w_in, np.cumsum(EVEN_SPLITS)[:-1].tolist(), axis=1)
    d = w_in.shape[0]
    z = lambda n: jnp.zeros((d, n), w_in.dtype)
    qr = jnp.pad(qa_rope.reshape(d, A_HEADS, A_ROPE), ((0, 0), (0, 0), (0, LANES - A_ROPE))).reshape(d, A_HEADS * LANES)
    cols = [qa_nope, iq, qb, kb, ik, z(LANES - IDX_DIM), qr, ka_rope, z(LANES - A_ROPE), c_kv, vb, iw, z(LANES - IDX_HEADS)]
    w = jnp.concatenate(cols, axis=1)
    assert w.shape[1] == EV_COLS
    return w.astype(BF16)


def _block_diag(w):
    h, a, b = w.shape
    eye = jnp.eye(h, dtype=w.dtype)
    return (eye[:, None, :, None] * w[:, :, None, :]).reshape(h * a, h * b)


def _even_proj_kernel(x_ref, g_ref, w_ref, wuk_ref, kvg_ref, t64_ref, t32_ref,
                      qc_ref, kc_ref, cvt_ref, iq_ref, ik_ref, iwt_ref, qb_ref, kb_ref, vbt_ref):
    h = _rms_rows(x_ref[...], g_ref[...]).astype(BF16)

    def proj(seg):
        return _dot(h, w_ref[:, seg[0]:seg[1]])

    q_lat = _dot(proj(EV_QN).astype(BF16), wuk_ref[...]) * (A_SCALE * LOG2E)
    for hd in range(A_HEADS):
        qc_ref[hd, :, 0:LANES] = q_lat[:, LANES * hd:LANES * (hd + 1)].astype(BF16)

    cos, s_hi, s_lo = t64_ref[0], t64_ref[1], t64_ref[2]
    y = proj(EV_R64)
    half = LANES // 2
    for blk in range((EV_R64[1] - EV_R64[0]) // LANES):
        r = _rope_block(y[:, LANES * blk:LANES * (blk + 1)], cos, s_hi, s_lo, IDX_DIM // 2)
        if blk < 4:
            r = (r * IDX_DIM ** -0.5).astype(BF16)
            iq_ref[2 * blk] = r[:, :half]
            iq_ref[2 * blk + 1] = r[:, half:]
        elif blk < 8:
            r = (r * (B_QK_DIM ** -0.5 * LOG2E)).astype(BF16)
            qb_ref[2 * (blk - 4)] = r[:, :half]
            qb_ref[2 * (blk - 4) + 1] = r[:, half:]
        elif blk < 12:
            r = r.astype(BF16)
            kb_ref[2 * (blk - 8)] = r[:, :half]
            kb_ref[2 * (blk - 8) + 1] = r[:, half:]
        else:
            ik_ref[...] = r[:, :half].astype(BF16)

    cos, s_hi, s_lo = t32_ref[0], t32_ref[1], t32_ref[2]
    y = proj(EV_R32)
    for blk in range(A_HEADS + 1):
        r = _rope_block(y[:, LANES * blk:LANES * (blk + 1)], cos, s_hi, s_lo, A_ROPE // 2)
        if blk < A_HEADS:
            qc_ref[blk, :, LANES:2 * LANES] = (r * (A_SCALE * LOG2E)).astype(BF16)
        else:
            kc_ref[:, LANES:2 * LANES] = r.astype(BF16)

    c_kv = _rms_rows(proj(EV_CKV), kvg_ref[...])
    kc_ref[:, 0:LANES] = c_kv.astype(BF16)
    cvt_ref[...] = c_kv.T.astype(BF16)
    vbt_ref[...] = proj(EV_VB).T.astype(BF16)
    iwt_ref[...] = (proj(EV_IW) * IDX_HEADS ** -0.5).T[0:IDX_HEADS, :]


def _even_proj(x2, gain, w, wuk_bd, kv_gain, t64, t32, seq):
    n = x2.shape[0]
    tm = PROJ_TM
    tiles_per_seq = seq // tm
    row = lambda i: (i, 0)
    col = lambda i: (0, i)
    hrow = lambda i: (0, i, 0)
    full2 = lambda i: (0, 0)
    tab = lambda i: (0, i % tiles_per_seq, 0)
    out_shape = (
        jax.ShapeDtypeStruct((A_HEADS, n, 2 * LANES), BF16),
        jax.ShapeDtypeStruct((n, 2 * LANES), BF16),
        jax.ShapeDtypeStruct((A_KV_RANK, n), BF16),
        jax.ShapeDtypeStruct((IDX_HEADS, n, IDX_DIM), BF16),
        jax.ShapeDtypeStruct((n, IDX_DIM), BF16),
        jax.ShapeDtypeStruct((IDX_HEADS, n), F32),
        jax.ShapeDtypeStruct((2 * B_HEADS, n, B_QK_DIM), BF16),
        jax.ShapeDtypeStruct((2 * B_HEADS, n, B_QK_DIM), BF16),
        jax.ShapeDtypeStruct((B_HEADS * B_V_DIM, n), BF16),
    )
    return pl.pallas_call(
        _even_proj_kernel,
        grid=(n // tm,),
        in_specs=[
            pl.BlockSpec((tm, D_MODEL), row),
            pl.BlockSpec((1, D_MODEL), full2),
            pl.BlockSpec((D_MODEL, EV_COLS), full2),
            pl.BlockSpec((A_HEADS * A_NOPE, A_HEADS * A_KV_RANK), full2),
            pl.BlockSpec((1, A_KV_RANK), full2),
            pl.BlockSpec((3, tm, LANES), tab),
            pl.BlockSpec((3, tm, LANES), tab),
        ],
        out_specs=(
            pl.BlockSpec((A_HEADS, tm, 2 * LANES), hrow),
            pl.BlockSpec((tm, 2 * LANES), row),
            pl.BlockSpec((A_KV_RANK, tm), col),
            pl.BlockSpec((IDX_HEADS, tm, IDX_DIM), hrow),
            pl.BlockSpec((tm, IDX_DIM), row),
            pl.BlockSpec((IDX_HEADS, tm), col),
            pl.BlockSpec((2 * B_HEADS, tm, B_QK_DIM), hrow),
            pl.BlockSpec((2 * B_HEADS, tm, B_QK_DIM), hrow),
            pl.BlockSpec((B_HEADS * B_V_DIM, tm), col),
        ),
        out_shape=out_shape,
        compiler_params=_cparams(("parallel",)),
        name="even_proj",
    )(x2, gain, w, wuk_bd, kv_gain, t64, t32)


def _dsa_kernel(iq_ref, ik_ref, iwt_ref, qc_ref, kc_ref, cvt_ref, o_ref,
                keys_ref, hi_ref, lo_ref, lo2_ref, jstar_ref, sa_ref, sb_ref, *, k_sel, seq):
    i = pl.program_id(1)
    qb, kc = A_QB, KEY_CHUNK
    n_chunks = (i * qb + qb + kc - 1) // kc
    t_lane = i * qb + lax.broadcasted_iota(jnp.int32, (1, qb), 1)
    sub_pos = lax.broadcasted_iota(jnp.int32, (kc, 1), 0)
    iwt = iwt_ref[...]

    iq = iq_ref[...].reshape(IDX_HEADS * qb, IDX_DIM)

    def index_chunk(c):
        k = ik_ref[pl.ds(pl.multiple_of(c * kc, kc), kc), :]
        rel_all = jnp.maximum(_dot_nt(k, iq), 0.0)
        score = None
        for hd in range(IDX_HEADS):
            rel = rel_all[:, qb * hd:qb * (hd + 1)] * iwt[hd:hd + 1, :]
            score = rel if score is None else score + rel
        bits = lax.bitcast_convert_type(score, jnp.int32)
        key = jnp.where(bits < 0, bits ^ 0x7FFFFFFF, bits)
        key = jnp.where(score == 0.0, 0, key)
        key = jnp.where(c * kc + sub_pos <= t_lane, key, KEY_NEG_INF)
        keys_ref[c] = key
        hi_ref[c] = (key >> 16).astype(jnp.int16)
        lo_ref[c] = ((key & 0xFFFF) - HALF16).astype(jnp.int16)

    def index_pair(pair, carry):
        index_chunk(2 * pair)
        index_chunk(jnp.minimum(2 * pair + 1, n_chunks - 1))
        return carry

    n_pairs = (n_chunks + 1) // 2
    lax.fori_loop(0, n_pairs, index_pair, 0)

    pad_chunk = hi_ref.shape[0] - 1
    hi_ref[pad_chunk] = jnp.full((kc, qb), -HALF16, jnp.int16)
    lo2_ref[pad_chunk] = jnp.full((kc, qb), -HALF16, jnp.int16)
    rows16 = 2 * SUBLANES

    one16, zero16 = jnp.int16(1), jnp.int16(0)

    def paired_sum16(hits):
        def fold(hit):
            parts = [hit[rows16 * r:rows16 * (r + 1), :] for r in range(kc // rows16)]
            while len(parts) > 1:
                parts = [a + b for a, b in zip(parts[0::2], parts[1::2])]
            return parts[0]

        def body(pair, acc):
            second = jnp.where(2 * pair + 1 < n_chunks, 2 * pair + 1, pad_chunk)
            return acc + fold(hits(2 * pair, 2 * pair)) + fold(hits(second, 2 * pair + 1))
        acc = lax.fori_loop(0, n_pairs, body, jnp.zeros((rows16, qb), jnp.int16))
        return jnp.sum(acc.astype(jnp.int32), axis=0, keepdims=True)

    def count16(ref, pred):
        return paired_sum16(lambda c, _: jnp.where(pred(ref[c]), one16, zero16))

    def kth_largest16(ref, k_row):
        u = jnp.where(count16(ref, lambda x: x >= jnp.int16(0)) >= k_row, 0, -HALF16)

        def bit(it, u):
            cand = u + jnp.left_shift(jnp.int32(1), 14 - it)
            c16 = cand.astype(jnp.int16)
            return jnp.where(count16(ref, lambda x: x >= c16) >= k_row, cand, u)
        return lax.fori_loop(0, 15, bit, u)

    v_hi = kth_largest16(hi_ref, k_sel)
    v_hi16 = v_hi.astype(jnp.int16)
    k_low = k_sel - count16(hi_ref, lambda x: x > v_hi16)

    def mask_low(c, carry):
        lo2_ref[c] = jnp.where(hi_ref[c] == v_hi16, lo_ref[c], jnp.int16(-HALF16))
        return carry

    lax.fori_loop(0, n_chunks, mask_low, 0)
    v_lo = kth_largest16(lo2_ref, k_low)
    v_lo16 = v_lo.astype(jnp.int16)
    v = v_hi * (2 * HALF16) + (v_lo + HALF16)
    count_ge = (k_sel - k_low) + count16(lo2_ref, lambda x: x >= v_lo16)

    jstar_ref[...] = jnp.full(jstar_ref.shape, seq - 1, jnp.int32)

    @pl.when(jnp.max(count_ge) > k_sel)
    def _():
        need = k_low - count16(lo2_ref, lambda x: x > v_lo16)

        def mark_tied(c, carry):
            lo2_ref[c] = jnp.where(hi_ref[c] == v_hi16, jnp.where(lo_ref[c] == v_lo16, one16, zero16), zero16)
            return carry

        lax.fori_loop(0, n_chunks, mark_tied, 0)
        lo2_ref[pad_chunk] = jnp.zeros((kc, qb), jnp.int16)
        pos16 = lax.broadcasted_iota(jnp.int32, (kc, qb), 0).astype(jnp.int16)

        def tied_below(cand):
            def hits(c, c_pos):
                local = jnp.clip(cand - c_pos * kc, 0, kc).astype(jnp.int16)
                return jnp.where(pos16 < local, lo2_ref[c], zero16)
            return paired_sum16(hits)

        def index_bit(it, x):
            cand = x + jnp.left_shift(jnp.int32(1), (seq.bit_length() - 2) - it)
            return jnp.where(tied_below(cand) < need, cand, x)

        x = lax.fori_loop(0, seq.bit_length() - 1, index_bit, jnp.zeros((1, qb), jnp.int32))
        jstar_ref[...] = jnp.broadcast_to(x, jstar_ref.shape)

    j_star = jstar_ref[0:1, :]

    q = qc_ref[...].reshape(A_HEADS * qb, 2 * LANES)

    def scores(c):
        cc = jnp.minimum(c, n_chunks - 1)
        kv = kc_ref[pl.ds(pl.multiple_of(cc * kc, kc), kc), :]
        pos = c * kc + sub_pos
        key = keys_ref[cc]
        key = jnp.where(pos > j_star, key - 1, key)
        bias = jnp.where(key >= v, jnp.where(pos <= t_lane, 0.0, MASKED), MASKED)
        return _dot_nt(kv, q) + jnp.concatenate([bias] * A_HEADS, axis=1)

    def values(c):
        cc = jnp.clip(c, 0, n_chunks - 1)
        return (cvt_ref[:, pl.ds(pl.multiple_of(cc * kc, kc), kc)],)

    (o,) = _attend_pairs(n_pairs, scores, values, sa_ref, sb_ref, A_KV_RANK, A_HEADS * qb)
    for hd in range(A_HEADS):
        o_ref[:, A_KV_RANK * hd:A_KV_RANK * (hd + 1)] = o[:, qb * hd:qb * (hd + 1)].T.astype(BF16)


def _dsa(iq, ik, iwt, qc, kc, cvt, batch, seq):
    n = ik.shape[0]
    nq = seq // A_QB
    k_sel = min(TOPK_MAX, seq // 4)
    assert seq & (seq - 1) == 0 and seq % KEY_CHUNK == 0 and k_sel <= KEY_CHUNK and A_QB % LANES == 0
    qrow = lambda b, i: (b * nq + i, 0)
    hqrow = lambda b, i: (0, b * nq + i, 0)
    brow = lambda b, i: (b, 0)
    return pl.pallas_call(
        functools.partial(_dsa_kernel, k_sel=k_sel, seq=seq),
        grid=(batch, nq),
        in_specs=[
            pl.BlockSpec((IDX_HEADS, A_QB, IDX_DIM), hqrow),
            pl.BlockSpec((seq, IDX_DIM), brow),
            pl.BlockSpec((IDX_HEADS, A_QB), lambda b, i: (0, b * nq + i)),
            pl.BlockSpec((A_HEADS, A_QB, 2 * LANES), hqrow),
            pl.BlockSpec((seq, 2 * LANES), brow),
            pl.BlockSpec((A_KV_RANK, seq), lambda b, i: (0, b)),
        ],
        out_specs=pl.BlockSpec((A_QB, A_HEADS * A_KV_RANK), qrow),
        out_shape=jax.ShapeDtypeStruct((n, A_HEADS * A_KV_RANK), BF16),
        scratch_shapes=[pltpu.VMEM((seq // KEY_CHUNK, KEY_CHUNK, A_QB), jnp.int32),
                        pltpu.VMEM((seq // KEY_CHUNK + 1, KEY_CHUNK, A_QB), jnp.int16),
                        pltpu.VMEM((seq // KEY_CHUNK, KEY_CHUNK, A_QB), jnp.int16),
                        pltpu.VMEM((seq // KEY_CHUNK + 1, KEY_CHUNK, A_QB), jnp.int16),
                        pltpu.VMEM((SUBLANES, A_QB), jnp.int32),
                        pltpu.VMEM((KEY_CHUNK, A_HEADS * A_QB), F32),
                        pltpu.VMEM((KEY_CHUNK, A_HEADS * A_QB), F32)],
        compiler_params=_cparams(("parallel", "arbitrary")),
        name="dsa",
    )(iq, ik, iwt, qc, kc, cvt)


def _diff_kernel(q_ref, k_ref, vt_ref, lam_ref, g_ref, o_ref, sa_ref, sb_ref, *, lam_init):
    i = pl.program_id(1)
    qb, kc = B_QB, KEY_CHUNK
    nsub = 2 * B_HEADS
    n_chunks = (i + 1) * (qb // kc)
    key_pos = lax.broadcasted_iota(jnp.int32, (kc, qb), 0)
    t_lane = i * qb + lax.broadcasted_iota(jnp.int32, (kc, qb), 1)

    def scores(c):
        start = pl.multiple_of(jnp.minimum(c, n_chunks - 1) * kc, kc)
        bias = jnp.where(c * kc + key_pos <= t_lane, 0.0, MASKED)
        return jnp.concatenate([_dot_nt(k_ref[j, pl.ds(start, kc), :], q_ref[j]) + bias for j in range(nsub)], axis=1)

    def values(c):
        start = pl.multiple_of(jnp.clip(c, 0, n_chunks - 1) * kc, kc)
        return tuple(vt_ref[B_V_DIM * hd:B_V_DIM * (hd + 1), pl.ds(start, kc)] for hd in range(B_HEADS))

    outs = _attend_pairs((n_chunks + 1) // 2, scores, values, sa_ref, sb_ref, B_V_DIM, 2 * qb)

    lam = lam_ref[...]
    lam_full = (jnp.exp(jnp.sum(lam[0:1] * lam[1:2], axis=-1, keepdims=True))
                - jnp.exp(jnp.sum(lam[2:3] * lam[3:4], axis=-1, keepdims=True)) + lam_init)
    gain_col = g_ref[...]
    for hd in range(B_HEADS):
        o = outs[hd]
        d = o[:, 0:qb] - lam_full * o[:, qb:2 * qb]
        d = d * lax.rsqrt(jnp.mean(d * d, axis=0, keepdims=True) + NORM_EPS) * gain_col * (1.0 - lam_init)
        o_ref[:, B_V_DIM * hd:B_V_DIM * (hd + 1)] = d.T.astype(BF16)


def _diff(qb, kb, vbt, lam, subln, lam_init, batch, seq):
    n = vbt.shape[1]
    nq = seq // B_QB
    assert B_QB % KEY_CHUNK == 0
    return pl.pallas_call(
        functools.partial(_diff_kernel, lam_init=lam_init),
        grid=(batch, nq),
        in_specs=[
            pl.BlockSpec((2 * B_HEADS, B_QB, B_QK_DIM), lambda b, i: (0, b * nq + i, 0)),
            pl.BlockSpec((2 * B_HEADS, seq, B_QK_DIM), lambda b, i: (0, b, 0)),
            pl.BlockSpec((B_HEADS * B_V_DIM, seq), lambda b, i: (0, b)),
            pl.BlockSpec((4, B_QK_DIM), lambda b, i: (0, 0)),
            pl.BlockSpec((B_V_DIM, 1), lambda b, i: (0, 0)),
        ],
        out_specs=pl.BlockSpec((B_QB, B_HEADS * B_V_DIM), lambda b, i: (b * nq + i, 0)),
        out_shape=jax.ShapeDtypeStruct((n, B_HEADS * B_V_DIM), BF16),
        scratch_shapes=[pltpu.VMEM((KEY_CHUNK, 2 * B_HEADS * B_QB), F32)] * 2,
        compiler_params=_cparams(("parallel", "arbitrary")),
        name="diff_attn",
    )(qb, kb, vbt, lam, subln)


def _even_out_kernel(x_ref, ol_ref, ob_ref, wuv_ref, wo_ref, y_ref):
    half = A_HEADS * A_V_DIM
    o_a = _dot(ol_ref[...], wuv_ref[...]).astype(BF16)
    y_ref[...] = x_ref[...] + _dot(o_a, wo_ref[0:half, :]) + _dot(ob_ref[...], wo_ref[half:, :])


def _even_out(x2, o_lat, o_b, wuv_bd, w_out):
    n = x2.shape[0]
    tm = PROJ_TM
    row = lambda i: (i, 0)
    full2 = lambda i: (0, 0)
    return pl.pallas_call(
        _even_out_kernel,
        grid=(n // tm,),
        in_specs=[
            pl.BlockSpec((tm, D_MODEL), row),
            pl.BlockSpec((tm, A_HEADS * A_KV_RANK), row),
            pl.BlockSpec((tm, B_HEADS * B_V_DIM), row),
            pl.BlockSpec((A_HEADS * A_KV_RANK, A_HEADS * A_V_DIM), full2),
            pl.BlockSpec((D_MODEL, D_MODEL), full2),
        ],
        out_specs=pl.BlockSpec((tm, D_MODEL), row),
        out_shape=jax.ShapeDtypeStruct((n, D_MODEL), F32),
        compiler_params=_cparams(("parallel",)),
        name="even_out",
    )(x2, o_lat, o_b, wuv_bd, w_out)


OD_R64 = (0, 1792)
OD_V = (1792, 2560)
OD_G = (2560, 3072)
OD_COLS = 3072
GATES_PER_GROUP = C_HPG * 3
KV_WIDTH = C_GROUPS * C_DIM


def _odd_weight(w_in):
    qc, kc, vc, ks, vs, kw, vw, gc = jnp.split(w_in, np.cumsum(ODD_SPLITS)[:-1].tolist(), axis=1)
    d = w_in.shape[0]
    gates = jnp.pad(gc.reshape(d, C_GROUPS, GATES_PER_GROUP), ((0, 0), (0, 0), (0, LANES - GATES_PER_GROUP)))
    w = jnp.concatenate([qc, kc, ks, kw, vc, vs, vw, gates.reshape(d, C_GROUPS * LANES)], axis=1)
    assert w.shape[1] == OD_COLS
    return w.astype(BF16)


def _odd_proj_kernel(x_ref, g_ref, w_ref, t64_ref, q_ref, kc_ref, ks_ref, kw_ref, vc_ref, vst_ref, vwt_ref, gt_ref):
    h = _rms_rows(x_ref[...], g_ref[...]).astype(BF16)
    half = LANES // 2
    cos, s_hi, s_lo = t64_ref[0], t64_ref[1], t64_ref[2]
    y = _dot(h, w_ref[:, OD_R64[0]:OD_R64[1]])
    for blk in range((OD_R64[1] - OD_R64[0]) // LANES):
        r = _rope_block(y[:, LANES * blk:LANES * (blk + 1)], cos, s_hi, s_lo, C_DIM // 2)
        if blk < 8:
            r = (r * (C_DIM ** -0.5 * LOG2E)).astype(BF16)
            dst, j = q_ref, blk
        else:
            r = r.astype(BF16)
            dst, j = (kc_ref, ks_ref, kw_ref)[(blk - 8) // 2], (blk - 8) % 2
        dst[2 * j] = r[:, :half]
        dst[2 * j + 1] = r[:, half:]
    y = _dot(h, w_ref[:, OD_V[0]:OD_V[1]])
    vc = y[:, 0:KV_WIDTH].astype(BF16)
    for g in range(C_GROUPS):
        vc_ref[g] = vc[:, C_DIM * g:C_DIM * (g + 1)]
    vst_ref[...] = y[:, KV_WIDTH:2 * KV_WIDTH].T.astype(BF16)
    vwt_ref[...] = y[:, 2 * KV_WIDTH:3 * KV_WIDTH].T.astype(BF16)
    y = jax.nn.sigmoid(_dot(h, w_ref[:, OD_G[0]:OD_G[1]]))
    for g in range(C_GROUPS):
        gt_ref[g] = y[:, LANES * g:LANES * (g + 1)]


def _odd_proj(x2, gain, w, t64, seq):
    n = x2.shape[0]
    tm = PROJ_TM
    tiles_per_seq = seq // tm
    row = lambda i: (i, 0)
    col = lambda i: (0, i)
    hrow = lambda i: (0, i, 0)
    full2 = lambda i: (0, 0)
    kv_shape = jax.ShapeDtypeStruct((C_GROUPS, n, C_DIM), BF16)
    kv_spec = pl.BlockSpec((C_GROUPS, tm, C_DIM), hrow)
    kvt_shape = jax.ShapeDtypeStruct((KV_WIDTH, n), BF16)
    kvt_spec = pl.BlockSpec((KV_WIDTH, tm), col)
    return pl.pallas_call(
        _odd_proj_kernel,
        grid=(n // tm,),
        in_specs=[
            pl.BlockSpec((tm, D_MODEL), row),
            pl.BlockSpec((1, D_MODEL), full2),
            pl.BlockSpec((D_MODEL, OD_COLS), full2),
            pl.BlockSpec((3, tm, LANES), lambda i: (0, i % tiles_per_seq, 0)),
        ],
        out_specs=(pl.BlockSpec((C_HEADS, tm, C_DIM), hrow),) + (kv_spec,) * 4 + (kvt_spec,) * 2
        + (pl.BlockSpec((C_GROUPS, tm, LANES), hrow),),
        out_shape=(jax.ShapeDtypeStruct((C_HEADS, n, C_DIM), BF16),) + (kv_shape,) * 4 + (kvt_shape,) * 2
        + (jax.ShapeDtypeStruct((C_GROUPS, n, LANES), F32),),
        compiler_params=_cparams(("parallel",)),
        name="odd_proj",
    )(x2, gain, w, t64)


def _gelu_tanh(x):
    return 0.5 * x * (1.0 + jnp.tanh(math.sqrt(2.0 / math.pi) * (x + 0.044715 * (x * x * x))))


def _compress_kernel(kch_ref, vch_ref, pe_ref, w1_ref, w2_ref, w2t_ref, kc_ref, vct_ref):
    rows = kch_ref.shape[2]

    def hidden(src, kv):
        ch = src[0, 0].astype(F32)
        first = _dot((ch + pe_ref[kv, 0:1, :]).astype(BF16), w1_ref[kv, 0])
        second = _dot((ch + pe_ref[kv, 1:2, :]).astype(BF16), w1_ref[kv, 1])
        return _gelu_tanh(first + pltpu.roll(second, rows - 1, 0)).astype(BF16)

    kc_ref[0, 0] = _dot(hidden(kch_ref, 0), w2_ref[...]).astype(BF16)
    vct_ref[0, 0] = _dot_nt(w2t_ref[...], hidden(vch_ref, 1)).astype(BF16)


def _compress(kc_raw, vc_raw, pe, w1, w2, batch, seq):
    nchunk = seq // CMP_STRIDE
    width = CMP_STRIDE * C_DIM
    kch = kc_raw.reshape(C_GROUPS, batch, nchunk, width)
    vch = vc_raw.reshape(C_GROUPS, batch, nchunk, width)
    pe2 = pe.reshape(2, 2, width)
    w1s = w1.reshape(2, 2, width, CMP_HIDDEN).astype(BF16)
    blk = lambda g, b: (g, b, 0, 0)
    full2 = lambda g, b: (0, 0)
    return pl.pallas_call(
        _compress_kernel,
        grid=(C_GROUPS, batch),
        in_specs=[
            pl.BlockSpec((1, 1, nchunk, width), blk),
            pl.BlockSpec((1, 1, nchunk, width), blk),
            pl.BlockSpec((2, 2, width), lambda g, b: (0, 0, 0)),
            pl.BlockSpec((2, 2, width, CMP_HIDDEN), lambda g, b: (0, 0, 0, 0)),
            pl.BlockSpec((CMP_HIDDEN, C_DIM), full2),
            pl.BlockSpec((C_DIM, CMP_HIDDEN), full2),
        ],
        out_specs=(pl.BlockSpec((1, 1, nchunk, C_DIM), blk), pl.BlockSpec((1, 1, C_DIM, nchunk), blk)),
        out_shape=(jax.ShapeDtypeStruct((C_GROUPS, batch, nchunk, C_DIM), BF16),
                   jax.ShapeDtypeStruct((C_GROUPS, batch, C_DIM, nchunk), BF16)),
        compiler_params=_cparams(("parallel", "parallel")),
        name="nsa_compress",
    )(kch, vch, pe2, w1s, w2[0].astype(BF16), w2[1].T.astype(BF16))


def _split3(x):
    a = x.astype(BF16)
    r = x - a.astype(F32)
    b = r.astype(BF16)
    c = (r - b.astype(F32)).astype(BF16)
    return a, b, c


def _nsa_kernel(q_ref, kc_ref, vct_ref, ks_ref, vst_ref, kw_ref, vwt_ref, gt_ref, o_ref, *stage, n_blk, top_n):
    sa_ref, sb_ref = stage[0], stage[1]
    i = pl.program_id(2)
    qb, kc = C_QB, KEY_CHUNK
    q = q_ref[...].reshape(C_HPG * qb, C_DIM)
    heads = lambda a: jnp.concatenate([a] * C_HPG, axis=1)
    t_lane = i * qb + lax.broadcasted_iota(jnp.int32, (1, qb), 1)
    sub_pos = lax.broadcasted_iota(jnp.int32, (kc, 1), 0)

    assert qb % kc == 0 and WINDOW % kc == 0 and len(stage) == (qb + WINDOW) // kc
    first = i * (qb // kc) - WINDOW // kc
    dist0 = (lax.broadcasted_iota(jnp.int32, (kc, qb), 1) - lax.broadcasted_iota(jnp.int32, (kc, qb), 0)) + WINDOW
    for step in range(len(stage)):
        c = first + step
        exists = jnp.where(c >= 0, 0.0, MASKED)
        dist = dist0 - step * kc
        bias = jnp.where(dist >= 0, jnp.where(dist < WINDOW, exists, MASKED), MASKED)
        start = pl.multiple_of(jnp.maximum(c, 0) * kc, kc)
        stage[step][...] = _dot_nt(kw_ref[0, pl.ds(start, kc), :], q) + heads(bias)
    win = _flash_init(C_DIM, C_HPG * qb)
    for step in range(len(stage)):
        start = pl.multiple_of(jnp.maximum(first + step, 0) * kc, kc)
        win = _flash_update(stage[step][...], vwt_ref[:, pl.ds(start, kc)], win)
    o_win = _flash_result(win, C_DIM)

    n_cmp = kc_ref.shape[2]
    cmp_end = lax.broadcasted_iota(jnp.int32, (n_cmp, 1), 0) * CMP_STRIDE + (CMP_LEN - 1)
    vis = heads(jnp.where(cmp_end <= t_lane, 1.0, 0.0))
    s = _dot_nt(kc_ref[0, 0], q) + (vis - 1.0) * (-MASKED)
    e = jnp.exp2(s - jnp.max(s, axis=0, keepdims=True)) * vis
    den = jnp.sum(e, axis=0, keepdims=True)
    p_cmp = e / jnp.where(den > 0, den, 1.0)
    o_cmp = _dot(vct_ref[0, 0], p_cmp.astype(BF16))

    p_sum = p_cmp[:, 0:qb]
    for hd in range(1, C_HPG):
        p_sum = p_sum + p_cmp[:, qb * hd:qb * (hd + 1)]
    blk_id = lax.broadcasted_iota(jnp.int32, (n_blk, n_cmp), 0)
    cmp_id = lax.broadcasted_iota(jnp.int32, (n_blk, n_cmp), 1)
    ratio = SEL_BLOCK // CMP_STRIDE
    overlap_t = ((cmp_id < ratio * (blk_id + 1)) & (cmp_id + CMP_LEN // CMP_STRIDE > ratio * blk_id))
    overlap_t = jnp.where(overlap_t, 1.0, 0.0).astype(BF16)
    imp = sum(_dot(overlap_t, part) for part in _split3(p_sum))
    j_sub = lax.broadcasted_iota(jnp.int32, (n_blk, 1), 0)
    cur = t_lane // SEL_BLOCK
    forced = (j_sub == 0) | (j_sub == cur) | (j_sub == cur - 1)
    imp = jnp.where(forced, jnp.inf, imp)
    imp = jnp.where(j_sub * SEL_BLOCK <= t_lane, imp, -jnp.inf)
    ranks = []
    for tile in range(n_blk // SUBLANES):
        mine = imp[SUBLANES * tile:SUBLANES * (tile + 1), :]
        j_tile = j_sub[SUBLANES * tile:SUBLANES * (tile + 1), :]
        rank = jnp.zeros((SUBLANES, qb), jnp.int32)
        for other in range(n_blk):
            row = imp[other:other + 1, :]
            ge = jnp.where(row >= mine, 1, 0)
            gt = jnp.where(row > mine, 1, 0)
            if other < SUBLANES * tile:
                rank = rank + ge
            elif other >= SUBLANES * (tile + 1):
                rank = rank + gt
            else:
                rank = rank + jnp.where(j_tile > other, ge, gt)
        ranks.append(rank)
    sel_bias = jnp.where(jnp.concatenate(ranks, axis=0) < top_n, 0.0, MASKED).astype(BF16)

    key_blk = lax.broadcasted_iota(jnp.int32, (kc, n_blk), 0) // SEL_BLOCK
    blk_col = lax.broadcasted_iota(jnp.int32, (kc, n_blk), 1)

    n_slc = (i + 1) * (qb // kc)

    def slc_scores(c):
        cc = jnp.minimum(c, n_slc - 1)
        expand = jnp.where(blk_col == key_blk + cc * (kc // SEL_BLOCK), 1.0, 0.0).astype(BF16)
        block_bias = _dot(expand, sel_bias)
        bias = jnp.where(c * kc + sub_pos <= t_lane, block_bias, MASKED)
        k = ks_ref[0, pl.ds(pl.multiple_of(cc * kc, kc), kc), :]
        return _dot_nt(k, q) + heads(bias)

    def slc_values(c):
        cc = jnp.clip(c, 0, n_slc - 1)
        return (vst_ref[:, pl.ds(pl.multiple_of(cc * kc, kc), kc)],)

    (o_slc,) = _attend_pairs((n_slc + 1) // 2, slc_scores, slc_values, sa_ref, sb_ref, C_DIM, C_HPG * qb)

    gates_t = gt_ref[0].T
    cols = []
    for hd in range(C_HPG):
        g0, g1, g2 = (gates_t[3 * hd + j:3 * hd + j + 1, :] for j in range(3))
        blk = slice(qb * hd, qb * (hd + 1))
        cols.append(g0 * o_cmp[:, blk] + g1 * o_slc[:, blk] + g2 * o_win[:, blk])
    o_ref[...] = jnp.concatenate(cols, axis=0).T.astype(BF16)


def _nsa(q, k_cmp, v_cmp_t, ks, vst, kw, vwt, gates, batch, seq):
    n = ks.shape[1]
    nq = seq // C_QB
    n_blk = seq // SEL_BLOCK
    top_n = min(SEL_TOPN, n_blk)
    nchunk = seq // CMP_STRIDE
    k_spec = pl.BlockSpec((1, seq, C_DIM), lambda b, g, i: (g, b, 0))
    vt_spec = pl.BlockSpec((C_DIM, seq), lambda b, g, i: (g, b))
    return pl.pallas_call(
        functools.partial(_nsa_kernel, n_blk=n_blk, top_n=top_n),
        grid=(batch, C_GROUPS, nq),
        in_specs=[
            pl.BlockSpec((C_HPG, C_QB, C_DIM), lambda b, g, i: (g, b * nq + i, 0)),
            pl.BlockSpec((1, 1, nchunk, C_DIM), lambda b, g, i: (g, b, 0, 0)),
            pl.BlockSpec((1, 1, C_DIM, nchunk), lambda b, g, i: (g, b, 0, 0)),
            k_spec, vt_spec, k_spec, vt_spec,
            pl.BlockSpec((1, C_QB, LANES), lambda b, g, i: (g, b * nq + i, 0)),
        ],
        out_specs=pl.BlockSpec((C_QB, C_HPG * C_DIM), lambda b, g, i: (b * nq + i, g)),
        out_shape=jax.ShapeDtypeStruct((n, C_HEADS * C_DIM), BF16),
        scratch_shapes=[pltpu.VMEM((KEY_CHUNK, C_HPG * C_QB), F32)] * ((C_QB + WINDOW) // KEY_CHUNK),
        compiler_params=_cparams(("parallel", "parallel", "arbitrary")),
        name="nsa",
    )(q, k_cmp, v_cmp_t, ks, vst, kw, vwt, gates)


def _odd_out_kernel(x_ref, o_ref, wo_ref, y_ref):
    y_ref[...] = x_ref[...] + _dot(o_ref[...], wo_ref[...])


def _odd_out(x2, o_c, w_out):
    n = x2.shape[0]
    tm = PROJ_TM
    row = lambda i: (i, 0)
    return pl.pallas_call(
        _odd_out_kernel,
        grid=(n // tm,),
        in_specs=[
            pl.BlockSpec((tm, D_MODEL), row),
            pl.BlockSpec((tm, D_MODEL), row),
            pl.BlockSpec((D_MODEL, D_MODEL), lambda i: (0, 0)),
        ],
        out_specs=pl.BlockSpec((tm, D_MODEL), row),
        out_shape=jax.ShapeDtypeStruct((n, D_MODEL), F32),
        compiler_params=_cparams(("parallel",)),
        name="odd_out",
    )(x2, o_c, w_out)


FFN_CHUNKS = 2
FFN_TM = 512


def _ffn_kernel(x_ref, g_ref, wg_ref, wu_ref, wd_ref, gf_ref, y_ref, h_ref, acc_ref, *, final_norm):
    f = pl.program_id(1)

    @pl.when(f == 0)
    def _():
        h_ref[...] = _rms_rows(x_ref[...], g_ref[...]).astype(BF16)

    h = h_ref[...]
    gate = _dot(h, wg_ref[...])
    act = (gate * jax.nn.sigmoid(gate) * _dot(h, wu_ref[...])).astype(BF16)
    part = _dot(act, wd_ref[...])

    @pl.when(f == 0)
    def _():
        acc_ref[...] = part

    @pl.when(f == FFN_CHUNKS - 1)
    def _():
        y = x_ref[...] + acc_ref[...] + part if FFN_CHUNKS > 1 else x_ref[...] + part
        y_ref[...] = _rms_rows(y, gf_ref[...]) if final_norm else y


def _ffn(x2, gain, wg, wu, wd, g_final, final_norm):
    n = x2.shape[0]
    tm = FFN_TM
    fc = D_FF // FFN_CHUNKS
    assert FFN_CHUNKS in (1, 2) and fc * FFN_CHUNKS == D_FF and fc % LANES == 0 and n % tm == 0
    row = lambda i, f: (i, 0)
    full2 = lambda i, f: (0, 0)
    return pl.pallas_call(
        functools.partial(_ffn_kernel, final_norm=final_norm),
        grid=(n // tm, FFN_CHUNKS),
        in_specs=[
            pl.BlockSpec((tm, D_MODEL), row),
            pl.BlockSpec((1, D_MODEL), full2),
            pl.BlockSpec((D_MODEL, fc), lambda i, f: (0, f)),
            pl.BlockSpec((D_MODEL, fc), lambda i, f: (0, f)),
            pl.BlockSpec((fc, D_MODEL), lambda i, f: (f, 0)),
            pl.BlockSpec((1, D_MODEL), full2),
        ],
        out_specs=pl.BlockSpec((tm, D_MODEL), row),
        out_shape=jax.ShapeDtypeStruct((n, D_MODEL), F32),
        scratch_shapes=[pltpu.VMEM((tm, D_MODEL), BF16), pltpu.VMEM((tm, D_MODEL), F32)],
        compiler_params=_cparams(("parallel", "arbitrary")),
        name="ffn",
    )(x2, gain, wg, wu, wd, g_final)


def kernel(x, norm_mix, norm_ffn, norm_final, ev_w_in, ev_kv_gain, ev_w_uk, ev_w_uv, ev_lambda, ev_subln, ev_w_out,
           od_w_in, od_cmp_pe, od_cmp_w1, od_cmp_w2, od_w_out, ffn_w_gate, ffn_w_up, ffn_w_down):
    batch, seq, d = x.shape
    depth = norm_mix.shape[0]
    x2 = x.reshape(batch * seq, d)
    t64 = _rope_tables(seq, 64)
    t32 = _rope_tables(seq, A_ROPE)
    g_final = norm_final.reshape(1, d)
    for layer in range(depth):
        j = layer // 2
        gain = norm_mix[layer].reshape(1, d)
        if layer % 2 == 0:
            qc, kc, cvt, iq, ik, iwt, qb, kb, vbt = _even_proj(
                x2, gain, _even_weight(ev_w_in[j]), _block_diag(ev_w_uk[j]).astype(BF16),
                ev_kv_gain[j].reshape(1, A_KV_RANK), t64, t32, seq)
            o_lat = _dsa(iq, ik, iwt, qc, kc, cvt, batch, seq)
            lam_init = 0.8 - 0.6 * math.exp(-0.3 * layer)
            o_b = _diff(qb, kb, vbt, ev_lambda[j], ev_subln[j].reshape(B_V_DIM, 1), lam_init, batch, seq)
            x2 = _even_out(x2, o_lat, o_b, _block_diag(ev_w_uv[j]).astype(BF16), ev_w_out[j].astype(BF16))
        else:
            q, kc_raw, ks, kw, vc_raw, vst, vwt, gates = _odd_proj(x2, gain, _odd_weight(od_w_in[j]), t64, seq)
            k_cmp, v_cmp_t = _compress(kc_raw, vc_raw, od_cmp_pe[j], od_cmp_w1[j], od_cmp_w2[j], batch, seq)
            o_c = _nsa(q, k_cmp, v_cmp_t, ks, vst, kw, vwt, gates, batch, seq)
            x2 = _odd_out(x2, o_c, od_w_out[j].astype(BF16))
        x2 = _ffn(x2, norm_ffn[layer].reshape(1, d), ffn_w_gate[layer].astype(BF16), ffn_w_up[layer].astype(BF16),
                  ffn_w_down[layer].astype(BF16), g_final, layer == depth - 1)
    return x2.reshape(batch, seq, d)
```

```python
import functools
import math

import numpy as np
import jax
import jax.numpy as jnp
from jax import lax
from jax.experimental import pallas as pl
from jax.experimental.pallas import tpu as pltpu

F32 = jnp.float32
BF16 = jnp.bfloat16

D_MODEL = 1024
ROPE_THETA = 10000.0
NORM_EPS = 1e-6

A_HEADS = 8
A_NOPE = 64
A_ROPE = 32
A_KV_RANK = 128
A_V_DIM = 64
A_SCALE = (A_NOPE + A_ROPE) ** -0.5
IDX_HEADS = 8
IDX_DIM = 64
TOPK_MAX = 256

B_HEADS = 4
B_QK_DIM = 64
B_V_DIM = 2 * B_QK_DIM

C_HEADS = 16
C_GROUPS = 4
C_HPG = C_HEADS // C_GROUPS
C_DIM = 64
CMP_LEN = 32
CMP_STRIDE = 16
CMP_HIDDEN = 128
SEL_BLOCK = 64
SEL_TOPN = 16
WINDOW = 512

D_FF = -(-8 * D_MODEL // (3 * 256)) * 256

EVEN_SPLITS = [A_HEADS * A_NOPE, A_HEADS * A_ROPE, A_KV_RANK, A_ROPE, IDX_HEADS * IDX_DIM, IDX_DIM, IDX_HEADS,
               B_HEADS * 2 * B_QK_DIM, B_HEADS * 2 * B_QK_DIM, B_HEADS * B_V_DIM]
ODD_SPLITS = [C_HEADS * C_DIM] + [C_GROUPS * C_DIM] * 6 + [C_HEADS * 3]

LANES = 128
SUBLANES = 8
VMEM_LIMIT = 56 * 1024 * 1024
MASKED = -1e30
KEY_NEG_INF = -2139095041
HALF16 = 1 << 15
LOG2E = math.log2(math.e)

PROJ_TM = 512
A_QB = 512
KEY_CHUNK = 256
B_QB = 512
C_QB = 256


def _cparams(sem):
    return pltpu.CompilerParams(dimension_semantics=sem, vmem_limit_bytes=VMEM_LIMIT)


def _dot(a, b):
    return jnp.dot(a, b, preferred_element_type=F32)


def _dot_nt(a, b):
    return lax.dot_general(a, b, (((1,), (1,)), ((), ())), preferred_element_type=F32)


def _rms_rows(x, gain):
    return x * lax.rsqrt(jnp.mean(x * x, axis=-1, keepdims=True) + NORM_EPS) * gain


def _rope_block(y, cos, s_hi, s_lo, half):
    return y * cos + pltpu.roll(y, half, 1) * s_hi + pltpu.roll(y, LANES - half, 1) * s_lo


def _rope_tables(seq, d):
    half = d // 2
    pos = jnp.arange(seq, dtype=jnp.int32)
    inv = ROPE_THETA ** (-jnp.arange(0, d, 2, dtype=F32) / d)
    ang = pos.astype(F32)[:, None] * inv[None, :]
    cos, sin = jnp.cos(ang), jnp.sin(ang)
    lane = np.arange(LANES)
    idx = (lane % d) % half
    first = jnp.asarray((lane % d) < half)
    c = cos[:, idx]
    s = sin[:, idx]
    return jnp.stack([c, jnp.where(first[None, :], 0.0, s), jnp.where(first[None, :], -s, 0.0)])


ONES_ROWS = 16


def _flash_probs(s, m):
    m_new = jnp.maximum(m, jnp.max(s, axis=0, keepdims=True))
    return m_new, jnp.exp2(m - m_new), jnp.exp2(s - m_new).astype(BF16)


def _flash_accumulate(acc, alpha, vt, p):
    vt_ones = jnp.concatenate([vt, jnp.ones((ONES_ROWS, vt.shape[1]), BF16)], axis=0)
    return alpha * acc + _dot(vt_ones, p)


def _flash_update(s, vt, state):
    m, acc = state
    m_new, alpha, p = _flash_probs(s, m)
    return m_new, _flash_accumulate(acc, alpha, vt, p)


def _attend_pairs(n_pairs, scores, values, sa_ref, sb_ref, dv, group_lanes):
    groups = sa_ref.shape[1] // group_lanes

    def update(s_ref, c, states):
        vts = values(c)
        return tuple(_flash_update(s_ref[:, group_lanes * g:group_lanes * (g + 1)], vts[g], states[g])
                     for g in range(groups))

    sa_ref[...] = scores(0)

    def trip(pair, states):
        sb_ref[...] = scores(2 * pair + 1)
        states = update(sa_ref, 2 * pair, states)
        sa_ref[...] = scores(2 * pair + 2)
        return update(sb_ref, 2 * pair + 1, states)

    states = lax.fori_loop(0, n_pairs, trip, tuple(_flash_init(dv, group_lanes) for _ in range(groups)))
    return tuple(_flash_result(st, dv) for st in states)


def _flash_init(dv, lanes):
    return (jnp.full((1, lanes), MASKED, F32), jnp.zeros((dv + ONES_ROWS, lanes), F32))


def _flash_result(state, dv):
    _, acc = state
    return acc[0:dv] / acc[dv:dv + 1]


EV_QN = (0, 512)
EV_R64 = (512, 2176)
EV_R32 = (2176, 3328)
EV_CKV = (3328, 3456)
EV_VB = (3456, 3968)
EV_IW = (3968, 4096)
EV_COLS = 4096


def _even_weight(w_in):
    qa_nope, qa_rope, c_kv, ka_rope, iq, ik, iw, qb, kb, vb = jnp.split(w_in, np.cumsum(EVEN_SPLITS)[:-1].tolist(), axis=1)
    d = w_in.shape[0]
    z = lambda n: jnp.zeros((d, n), w_in.dtype)
    qr = jnp.pad(qa_rope.reshape(d, A_HEADS, A_ROPE), ((0, 0), (0, 0), (0, LANES - A_ROPE))).reshape(d, A_HEADS * LANES)
    cols = [qa_nope, iq, qb, kb, ik, z(LANES - IDX_DIM), qr, ka_rope, z(LANES - A_ROPE), c_kv, vb, iw, z(LANES - IDX_HEADS)]
    w = jnp.concatenate(cols, axis=1)
    assert w.shape[1] == EV_COLS
    return w.astype(BF16)


def _block_diag(w):
    h, a, b = w.shape
    eye = jnp.eye(h, dtype=w.dtype)
    return (eye[:, None, :, None] * w[:, :, None, :]).reshape(h * a, h * b)


def _even_proj_kernel(x_ref, g_ref, w_ref, wuk_ref, kvg_ref, t64_ref, t32_ref,
                      qc_ref, kc_ref, cvt_ref, iq_ref, ik_ref, iwt_ref, qb_ref, kb_ref, vbt_ref):
    h = _rms_rows(x_ref[...], g_ref[...]).astype(BF16)

    def proj(seg):
        return _dot(h, w_ref[:, seg[0]:seg[1]])

    q_lat = _dot(proj(EV_QN).astype(BF16), wuk_ref[...]) * (A_SCALE * LOG2E)
    for hd in range(A_HEADS):
        qc_ref[hd, :, 0:LANES] = q_lat[:, LANES * hd:LANES * (hd + 1)].astype(BF16)

    cos, s_hi, s_lo = t64_ref[0], t64_ref[1], t64_ref[2]
    y = proj(EV_R64)
    half = LANES // 2
    for blk in range((EV_R64[1] - EV_R64[0]) // LANES):
        r = _rope_block(y[:, LANES * blk:LANES * (blk + 1)], cos, s_hi, s_lo, IDX_DIM // 2)
        if blk < 4:
            r = (r * IDX_DIM ** -0.5).astype(BF16)
            iq_ref[2 * blk] = r[:, :half]
            iq_ref[2 * blk + 1] = r[:, half:]
        elif blk < 8:
            r = (r * (B_QK_DIM ** -0.5 * LOG2E)).astype(BF16)
            qb_ref[2 * (blk - 4)] = r[:, :half]
            qb_ref[2 * (blk - 4) + 1] = r[:, half:]
        elif blk < 12:
            r = r.astype(BF16)
            kb_ref[2 * (blk - 8)] = r[:, :half]
            kb_ref[2 * (blk - 8) + 1] = r[:, half:]
        else:
            ik_ref[...] = r[:, :half].astype(BF16)

    cos, s_hi, s_lo = t32_ref[0], t32_ref[1], t32_ref[2]
    y = proj(EV_R32)
    for blk in range(A_HEADS + 1):
        r = _rope_block(y[:, LANES * blk:LANES * (blk + 1)], cos, s_hi, s_lo, A_ROPE // 2)
        if blk < A_HEADS:
            qc_ref[blk, :, LANES:2 * LANES] = (r * (A_SCALE * LOG2E)).astype(BF16)
        else:
            kc_ref[:, LANES:2 * LANES] = r.astype(BF16)

    c_kv = _rms_rows(proj(EV_CKV), kvg_ref[...])
    kc_ref[:, 0:LANES] = c_kv.astype(BF16)
    cvt_ref[...] = c_kv.T.astype(BF16)
    vbt_ref[...] = proj(EV_VB).T.astype(BF16)
    iwt_ref[...] = (proj(EV_IW) * IDX_HEADS ** -0.5).T[0:IDX_HEADS, :]


def _even_proj(x2, gain, w, wuk_bd, kv_gain, t64, t32, seq):
    n = x2.shape[0]
    tm = PROJ_TM
    tiles_per_seq = seq // tm
    row = lambda i: (i, 0)
    col = lambda i: (0, i)
    hrow = lambda i: (0, i, 0)
    full2 = lambda i: (0, 0)
    tab = lambda i: (0, i % tiles_per_seq, 0)
    out_shape = (
        jax.ShapeDtypeStruct((A_HEADS, n, 2 * LANES), BF16),
        jax.ShapeDtypeStruct((n, 2 * LANES), BF16),
        jax.ShapeDtypeStruct((A_KV_RANK, n), BF16),
        jax.ShapeDtypeStruct((IDX_HEADS, n, IDX_DIM), BF16),
        jax.ShapeDtypeStruct((n, IDX_DIM), BF16),
        jax.ShapeDtypeStruct((IDX_HEADS, n), F32),
        jax.ShapeDtypeStruct((2 * B_HEADS, n, B_QK_DIM), BF16),
        jax.ShapeDtypeStruct((2 * B_HEADS, n, B_QK_DIM), BF16),
        jax.ShapeDtypeStruct((B_HEADS * B_V_DIM, n), BF16),
    )
    return pl.pallas_call(
        _even_proj_kernel,
        grid=(n // tm,),
        in_specs=[
            pl.BlockSpec((tm, D_MODEL), row),
            pl.BlockSpec((1, D_MODEL), full2),
            pl.BlockSpec((D_MODEL, EV_COLS), full2),
            pl.BlockSpec((A_HEADS * A_NOPE, A_HEADS * A_KV_RANK), full2),
            pl.BlockSpec((1, A_KV_RANK), full2),
            pl.BlockSpec((3, tm, LANES), tab),
            pl.BlockSpec((3, tm, LANES), tab),
        ],
        out_specs=(
            pl.BlockSpec((A_HEADS, tm, 2 * LANES), hrow),
            pl.BlockSpec((tm, 2 * LANES), row),
            pl.BlockSpec((A_KV_RANK, tm), col),
            pl.BlockSpec((IDX_HEADS, tm, IDX_DIM), hrow),
            pl.BlockSpec((tm, IDX_DIM), row),
            pl.BlockSpec((IDX_HEADS, tm), col),
            pl.BlockSpec((2 * B_HEADS, tm, B_QK_DIM), hrow),
            pl.BlockSpec((2 * B_HEADS, tm, B_QK_DIM), hrow),
            pl.BlockSpec((B_HEADS * B_V_DIM, tm), col),
        ),
        out_shape=out_shape,
        compiler_params=_cparams(("parallel",)),
        name="even_proj",
    )(x2, gain, w, wuk_bd, kv_gain, t64, t32)


def _dsa_kernel(iq_ref, ik_ref, iwt_ref, qc_ref, kc_ref, cvt_ref, o_ref,
                keys_ref, hi_ref, lo_ref, lo2_ref, jstar_ref, sa_ref, sb_ref, *, k_sel, seq):
    i = pl.program_id(1)
    qb, kc = A_QB, KEY_CHUNK
    n_chunks = (i * qb + qb + kc - 1) // kc
    t_lane = i * qb + lax.broadcasted_iota(jnp.int32, (1, qb), 1)
    sub_pos = lax.broadcasted_iota(jnp.int32, (kc, 1), 0)
    iwt = iwt_ref[...]

    iq = iq_ref[...].reshape(IDX_HEADS * qb, IDX_DIM)

    def index_chunk(c):
        k = ik_ref[pl.ds(pl.multiple_of(c * kc, kc), kc), :]
        rel_all = jnp.maximum(_dot_nt(k, iq), 0.0)
        score = None
        for hd in range(IDX_HEADS):
            rel = rel_all[:, qb * hd:qb * (hd + 1)] * iwt[hd:hd + 1, :]
            score = rel if score is None else score + rel
        bits = lax.bitcast_convert_type(score, jnp.int32)
        key = jnp.where(bits < 0, bits ^ 0x7FFFFFFF, bits)
        key = jnp.where(score == 0.0, 0, key)
        key = jnp.where(c * kc + sub_pos <= t_lane, key, KEY_NEG_INF)
        keys_ref[c] = key
        hi_ref[c] = (key >> 16).astype(jnp.int16)
        lo_ref[c] = ((key & 0xFFFF) - HALF16).astype(jnp.int16)

    def index_pair(pair, carry):
        index_chunk(2 * pair)
        index_chunk(jnp.minimum(2 * pair + 1, n_chunks - 1))
        return carry

    n_pairs = (n_chunks + 1) // 2
    lax.fori_loop(0, n_pairs, index_pair, 0)

    pad_chunk = hi_ref.shape[0] - 1
    hi_ref[pad_chunk] = jnp.full((kc, qb), -HALF16, jnp.int16)
    lo2_ref[pad_chunk] = jnp.full((kc, qb), -HALF16, jnp.int16)
    rows16 = 2 * SUBLANES

    one16, zero16 = jnp.int16(1), jnp.int16(0)

    def paired_sum16(hits):
        def fold(hit):
            parts = [hit[rows16 * r:rows16 * (r + 1), :] for r in range(kc // rows16)]
            while len(parts) > 1:
                parts = [a + b for a, b in zip(parts[0::2], parts[1::2])]
            return parts[0]

        def body(pair, acc):
            second = jnp.where(2 * pair + 1 < n_chunks, 2 * pair + 1, pad_chunk)
            return acc + fold(hits(2 * pair, 2 * pair)) + fold(hits(second, 2 * pair + 1))
        acc = lax.fori_loop(0, n_pairs, body, jnp.zeros((rows16, qb), jnp.int16))
        return jnp.sum(acc.astype(jnp.int32), axis=0, keepdims=True)

    def count16(ref, pred):
        return paired_sum16(lambda c, _: jnp.where(pred(ref[c]), one16, zero16))

    def kth_largest16(ref, k_row):
        u = jnp.where(count16(ref, lambda x: x >= jnp.int16(0)) >= k_row, 0, -HALF16)

        def bit(it, u):
            cand = u + jnp.left_shift(jnp.int32(1), 14 - it)
            c16 = cand.astype(jnp.int16)
            return jnp.where(count16(ref, lambda x: x >= c16) >= k_row, cand, u)
        return lax.fori_loop(0, 15, bit, u)

    v_hi = kth_largest16(hi_ref, k_sel)
    v_hi16 = v_hi.astype(jnp.int16)
    k_low = k_sel - count16(hi_ref, lambda x: x > v_hi16)

    def mask_low(c, carry):
        lo2_ref[c] = jnp.where(hi_ref[c] == v_hi16, lo_ref[c], jnp.int16(-HALF16))
        return carry

    lax.fori_loop(0, n_chunks, mask_low, 0)
    v_lo = kth_largest16(lo2_ref, k_low)
    v_lo16 = v_lo.astype(jnp.int16)
    v = v_hi * (2 * HALF16) + (v_lo + HALF16)
    count_ge = (k_sel - k_low) + count16(lo2_ref, lambda x: x >= v_lo16)

    jstar_ref[...] = jnp.full(jstar_ref.shape, seq - 1, jnp.int32)

    @pl.when(jnp.max(count_ge) > k_sel)
    def _():
        need = k_low - count16(lo2_ref, lambda x: x > v_lo16)

        def mark_tied(c, carry):
            lo2_ref[c] = jnp.where(hi_ref[c] == v_hi16, jnp.where(lo_ref[c] == v_lo16, one16, zero16), zero16)
            return carry

        lax.fori_loop(0, n_chunks, mark_tied, 0)
        lo2_ref[pad_chunk] = jnp.zeros((kc, qb), jnp.int16)
        pos16 = lax.broadcasted_iota(jnp.int32, (kc, qb), 0).astype(jnp.int16)

        def tied_below(cand):
            def hits(c, c_pos):
                local = jnp.clip(cand - c_pos * kc, 0, kc).astype(jnp.int16)
                return jnp.where(pos16 < local, lo2_ref[c], zero16)
            return paired_sum16(hits)

        def index_bit(it, x):
            cand = x + jnp.left_shift(jnp.int32(1), (seq.bit_length() - 2) - it)
            return jnp.where(tied_below(cand) < need, cand, x)

        x = lax.fori_loop(0, seq.bit_length() - 1, index_bit, jnp.zeros((1, qb), jnp.int32))
        jstar_ref[...] = jnp.broadcast_to(x, jstar_ref.shape)

    j_star = jstar_ref[0:1, :]

    q = qc_ref[...].reshape(A_HEADS * qb, 2 * LANES)

    def scores(c):
        cc = jnp.minimum(c, n_chunks - 1)
        kv = kc_ref[pl.ds(pl.multiple_of(cc * kc, kc), kc), :]
        pos = c * kc + sub_pos
        key = keys_ref[cc]
        key = jnp.where(pos > j_star, key - 1, key)
        bias = jnp.where(key >= v, jnp.where(pos <= t_lane, 0.0, MASKED), MASKED)
        return _dot_nt(kv, q) + jnp.concatenate([bias] * A_HEADS, axis=1)

    def values(c):
        cc = jnp.clip(c, 0, n_chunks - 1)
        return (cvt_ref[:, pl.ds(pl.multiple_of(cc * kc, kc), kc)],)

    (o,) = _attend_pairs(n_pairs, scores, values, sa_ref, sb_ref, A_KV_RANK, A_HEADS * qb)
    for hd in range(A_HEADS):
        o_ref[:, A_KV_RANK * hd:A_KV_RANK * (hd + 1)] = o[:, qb * hd:qb * (hd + 1)].T.astype(BF16)


def _dsa(iq, ik, iwt, qc, kc, cvt, batch, seq):
    n = ik.shape[0]
    nq = seq // A_QB
    k_sel = min(TOPK_MAX, seq // 4)
    assert seq & (seq - 1) == 0 and seq % KEY_CHUNK == 0 and k_sel <= KEY_CHUNK and A_QB % LANES == 0
    qrow = lambda b, i: (b * nq + i, 0)
    hqrow = lambda b, i: (0, b * nq + i, 0)
    brow = lambda b, i: (b, 0)
    return pl.pallas_call(
        functools.partial(_dsa_kernel, k_sel=k_sel, seq=seq),
        grid=(batch, nq),
        in_specs=[
            pl.BlockSpec((IDX_HEADS, A_QB, IDX_DIM), hqrow),
            pl.BlockSpec((seq, IDX_DIM), brow),
            pl.BlockSpec((IDX_HEADS, A_QB), lambda b, i: (0, b * nq + i)),
            pl.BlockSpec((A_HEADS, A_QB, 2 * LANES), hqrow),
            pl.BlockSpec((seq, 2 * LANES), brow),
            pl.BlockSpec((A_KV_RANK, seq), lambda b, i: (0, b)),
        ],
        out_specs=pl.BlockSpec((A_QB, A_HEADS * A_KV_RANK), qrow),
        out_shape=jax.ShapeDtypeStruct((n, A_HEADS * A_KV_RANK), BF16),
        scratch_shapes=[pltpu.VMEM((seq // KEY_CHUNK, KEY_CHUNK, A_QB), jnp.int32),
                        pltpu.VMEM((seq // KEY_CHUNK + 1, KEY_CHUNK, A_QB), jnp.int16),
                        pltpu.VMEM((seq // KEY_CHUNK, KEY_CHUNK, A_QB), jnp.int16),
                        pltpu.VMEM((seq // KEY_CHUNK + 1, KEY_CHUNK, A_QB), jnp.int16),
                        pltpu.VMEM((SUBLANES, A_QB), jnp.int32),
                        pltpu.VMEM((KEY_CHUNK, A_HEADS * A_QB), F32),
                        pltpu.VMEM((KEY_CHUNK, A_HEADS * A_QB), F32)],
        compiler_params=_cparams(("parallel", "arbitrary")),
        name="dsa",
    )(iq, ik, iwt, qc, kc, cvt)


def _diff_kernel(q_ref, k_ref, vt_ref, lam_ref, g_ref, o_ref, sa_ref, sb_ref, *, lam_init):
    i = pl.program_id(1)
    qb, kc = B_QB, KEY_CHUNK
    nsub = 2 * B_HEADS
    n_chunks = (i + 1) * (qb // kc)
    key_pos = lax.broadcasted_iota(jnp.int32, (kc, qb), 0)
    t_lane = i * qb + lax.broadcasted_iota(jnp.int32, (kc, qb), 1)

    def scores(c):
        start = pl.multiple_of(jnp.minimum(c, n_chunks - 1) * kc, kc)
        bias = jnp.where(c * kc + key_pos <= t_lane, 0.0, MASKED)
        return jnp.concatenate([_dot_nt(k_ref[j, pl.ds(start, kc), :], q_ref[j]) + bias for j in range(nsub)], axis=1)

    def values(c):
        start = pl.multiple_of(jnp.clip(c, 0, n_chunks - 1) * kc, kc)
        return tuple(vt_ref[B_V_DIM * hd:B_V_DIM * (hd + 1), pl.ds(start, kc)] for hd in range(B_HEADS))

    outs = _attend_pairs((n_chunks + 1) // 2, scores, values, sa_ref, sb_ref, B_V_DIM, 2 * qb)

    lam = lam_ref[...]
    lam_full = (jnp.exp(jnp.sum(lam[0:1] * lam[1:2], axis=-1, keepdims=True))
                - jnp.exp(jnp.sum(lam[2:3] * lam[3:4], axis=-1, keepdims=True)) + lam_init)
    gain_col = g_ref[...]
    for hd in range(B_HEADS):
        o = outs[hd]
        d = o[:, 0:qb] - lam_full * o[:, qb:2 * qb]
        d = d * lax.rsqrt(jnp.mean(d * d, axis=0, keepdims=True) + NORM_EPS) * gain_col * (1.0 - lam_init)
        o_ref[:, B_V_DIM * hd:B_V_DIM * (hd + 1)] = d.T.astype(BF16)


def _diff(qb, kb, vbt, lam, subln, lam_init, batch, seq):
    n = vbt.shape[1]
    nq = seq // B_QB
    assert B_QB % KEY_CHUNK == 0
    return pl.pallas_call(
        functools.partial(_diff_kernel, lam_init=lam_init),
        grid=(batch, nq),
        in_specs=[
            pl.BlockSpec((2 * B_HEADS, B_QB, B_QK_DIM), lambda b, i: (0, b * nq + i, 0)),
            pl.BlockSpec((2 * B_HEADS, seq, B_QK_DIM), lambda b, i: (0, b, 0)),
            pl.BlockSpec((B_HEADS * B_V_DIM, seq), lambda b, i: (0, b)),
            pl.BlockSpec((4, B_QK_DIM), lambda b, i: (0, 0)),
            pl.BlockSpec((B_V_DIM, 1), lambda b, i: (0, 0)),
        ],
        out_specs=pl.BlockSpec((B_QB, B_HEADS * B_V_DIM), lambda b, i: (b * nq + i, 0)),
        out_shape=jax.ShapeDtypeStruct((n, B_HEADS * B_V_DIM), BF16),
        scratch_shapes=[pltpu.VMEM((KEY_CHUNK, 2 * B_HEADS * B_QB), F32)] * 2,
        compiler_params=_cparams(("parallel", "arbitrary")),
        name="diff_attn",
    )(qb, kb, vbt, lam, subln)


def _even_out_kernel(x_ref, ol_ref, ob_ref, wuv_ref, wo_ref, y_ref):
    half = A_HEADS * A_V_DIM
    o_a = _dot(ol_ref[...], wuv_ref[...]).astype(BF16)
    y_ref[...] = x_ref[...] + _dot(o_a, wo_ref[0:half, :]) + _dot(ob_ref[...], wo_ref[half:, :])


def _even_out(x2, o_lat, o_b, wuv_bd, w_out):
    n = x2.shape[0]
    tm = PROJ_TM
    row = lambda i: (i, 0)
    full2 = lambda i: (0, 0)
    return pl.pallas_call(
        _even_out_kernel,
        grid=(n // tm,),
        in_specs=[
            pl.BlockSpec((tm, D_MODEL), row),
            pl.BlockSpec((tm, A_HEADS * A_KV_RANK), row),
            pl.BlockSpec((tm, B_HEADS * B_V_DIM), row),
            pl.BlockSpec((A_HEADS * A_KV_RANK, A_HEADS * A_V_DIM), full2),
            pl.BlockSpec((D_MODEL, D_MODEL), full2),
        ],
        out_specs=pl.BlockSpec((tm, D_MODEL), row),
        out_shape=jax.ShapeDtypeStruct((n, D_MODEL), F32),
        compiler_params=_cparams(("parallel",)),
        name="even_out",
    )(x2, o_lat, o_b, wuv_bd, w_out)


OD_R64 = (0, 1792)
OD_V = (1792, 2560)
OD_G = (2560, 3072)
OD_COLS = 3072
GATES_PER_GROUP = C_HPG * 3
KV_WIDTH = C_GROUPS * C_DIM


def _odd_weight(w_in):
    qc, kc, vc, ks, vs, kw, vw, gc = jnp.split(w_in, np.cumsum(ODD_SPLITS)[:-1].tolist(), axis=1)
    d = w_in.shape[0]
    gates = jnp.pad(gc.reshape(d, C_GROUPS, GATES_PER_GROUP), ((0, 0), (0, 0), (0, LANES - GATES_PER_GROUP)))
    w = jnp.concatenate([qc, kc, ks, kw, vc, vs, vw, gates.reshape(d, C_GROUPS * LANES)], axis=1)
    assert w.shape[1] == OD_COLS
    return w.astype(BF16)


def _odd_proj_kernel(x_ref, g_ref, w_ref, t64_ref, q_ref, kc_ref, ks_ref, kw_ref, vc_ref, vst_ref, vwt_ref, gt_ref):
    h = _rms_rows(x_ref[...], g_ref[...]).astype(BF16)
    half = LANES // 2
    cos, s_hi, s_lo = t64_ref[0], t64_ref[1], t64_ref[2]
    y = _dot(h, w_ref[:, OD_R64[0]:OD_R64[1]])
    for blk in range((OD_R64[1] - OD_R64[0]) // LANES):
        r = _rope_block(y[:, LANES * blk:LANES * (blk + 1)], cos, s_hi, s_lo, C_DIM // 2)
        if blk < 8:
            r = (r * (C_DIM ** -0.5 * LOG2E)).astype(BF16)
            dst, j = q_ref, blk
        else:
            r = r.astype(BF16)
            dst, j = (kc_ref, ks_ref, kw_ref)[(blk - 8) // 2], (blk - 8) % 2
        dst[2 * j] = r[:, :half]
        dst[2 * j + 1] = r[:, half:]
    y = _dot(h, w_ref[:, OD_V[0]:OD_V[1]])
    vc = y[:, 0:KV_WIDTH].astype(BF16)
    for g in range(C_GROUPS):
        vc_ref[g] = vc[:, C_DIM * g:C_DIM * (g + 1)]
    vst_ref[...] = y[:, KV_WIDTH:2 * KV_WIDTH].T.astype(BF16)
    vwt_ref[...] = y[:, 2 * KV_WIDTH:3 * KV_WIDTH].T.astype(BF16)
    y = jax.nn.sigmoid(_dot(h, w_ref[:, OD_G[0]:OD_G[1]]))
    for g in range(C_GROUPS):
        gt_ref[g] = y[:, LANES * g:LANES * (g + 1)]


def _odd_proj(x2, gain, w, t64, seq):
    n = x2.shape[0]
    tm = PROJ_TM
    tiles_per_seq = seq // tm
    row = lambda i: (i, 0)
    col = lambda i: (0, i)
    hrow = lambda i: (0, i, 0)
    full2 = lambda i: (0, 0)
    kv_shape = jax.ShapeDtypeStruct((C_GROUPS, n, C_DIM), BF16)
    kv_spec = pl.BlockSpec((C_GROUPS, tm, C_DIM), hrow)
    kvt_shape = jax.ShapeDtypeStruct((KV_WIDTH, n), BF16)
    kvt_spec = pl.BlockSpec((KV_WIDTH, tm), col)
    return pl.pallas_call(
        _odd_proj_kernel,
        grid=(n // tm,),
        in_specs=[
            pl.BlockSpec((tm, D_MODEL), row),
            pl.BlockSpec((1, D_MODEL), full2),
            pl.BlockSpec((D_MODEL, OD_COLS), full2),
            pl.BlockSpec((3, tm, LANES), lambda i: (0, i % tiles_per_seq, 0)),
        ],
        out_specs=(pl.BlockSpec((C_HEADS, tm, C_DIM), hrow),) + (kv_spec,) * 4 + (kvt_spec,) * 2
        + (pl.BlockSpec((C_GROUPS, tm, LANES), hrow),),
        out_shape=(jax.ShapeDtypeStruct((C_HEADS, n, C_DIM), BF16),) + (kv_shape,) * 4 + (kvt_shape,) * 2
        + (jax.ShapeDtypeStruct((C_GROUPS, n, LANES), F32),),
        compiler_params=_cparams(("parallel",)),
        name="odd_proj",
    )(x2, gain, w, t64)


def _gelu_tanh(x):
    return 0.5 * x * (1.0 + jnp.tanh(math.sqrt(2.0 / math.pi) * (x + 0.044715 * (x * x * x))))


def _compress_kernel(kch_ref, vch_ref, pe_ref, w1_ref, w2_ref, w2t_ref, kc_ref, vct_ref):
    rows = kch_ref.shape[2]

    def hidden(src, kv):
        ch = src[0, 0].astype(F32)
        first = _dot((ch + pe_ref[kv, 0:1, :]).astype(BF16), w1_ref[kv, 0])
        second = _dot((ch + pe_ref[kv, 1:2, :]).astype(BF16), w1_ref[kv, 1])
        return _gelu_tanh(first + pltpu.roll(second, rows - 1, 0)).astype(BF16)

    kc_ref[0, 0] = _dot(hidden(kch_ref, 0), w2_ref[...]).astype(BF16)
    vct_ref[0, 0] = _dot_nt(w2t_ref[...], hidden(vch_ref, 1)).astype(BF16)


def _compress(kc_raw, vc_raw, pe, w1, w2, batch, seq):
    nchunk = seq // CMP_STRIDE
    width = CMP_STRIDE * C_DIM
    kch = kc_raw.reshape(C_GROUPS, batch, nchunk, width)
    vch = vc_raw.reshape(C_GROUPS, batch, nchunk, width)
    pe2 = pe.reshape(2, 2, width)
    w1s = w1.reshape(2, 2, width, CMP_HIDDEN).astype(BF16)
    blk = lambda g, b: (g, b, 0, 0)
    full2 = lambda g, b: (0, 0)
    return pl.pallas_call(
        _compress_kernel,
        grid=(C_GROUPS, batch),
        in_specs=[
            pl.BlockSpec((1, 1, nchunk, width), blk),
            pl.BlockSpec((1, 1, nchunk, width), blk),
            pl.BlockSpec((2, 2, width), lambda g, b: (0, 0, 0)),
            pl.BlockSpec((2, 2, width, CMP_HIDDEN), lambda g, b: (0, 0, 0, 0)),
            pl.BlockSpec((CMP_HIDDEN, C_DIM), full2),
            pl.BlockSpec((C_DIM, CMP_HIDDEN), full2),
        ],
        out_specs=(pl.BlockSpec((1, 1, nchunk, C_DIM), blk), pl.BlockSpec((1, 1, C_DIM, nchunk), blk)),
        out_shape=(jax.ShapeDtypeStruct((C_GROUPS, batch, nchunk, C_DIM), BF16),
                   jax.ShapeDtypeStruct((C_GROUPS, batch, C_DIM, nchunk), BF16)),
        compiler_params=_cparams(("parallel", "parallel")),
        name="nsa_compress",
    )(kch, vch, pe2, w1s, w2[0].astype(BF16), w2[1].T.astype(BF16))


def _split3(x):
    a = x.astype(BF16)
    r = x - a.astype(F32)
    b = r.astype(BF16)
    c = (r - b.astype(F32)).astype(BF16)
    return a, b, c


def _nsa_kernel(q_ref, kc_ref, vct_ref, ks_ref, vst_ref, kw_ref, vwt_ref, gt_ref, o_ref, *stage, n_blk, top_n):
    sa_ref, sb_ref = stage[0], stage[1]
    i = pl.program_id(2)
    qb, kc = C_QB, KEY_CHUNK
    q = q_ref[...].reshape(C_HPG * qb, C_DIM)
    heads = lambda a: jnp.concatenate([a] * C_HPG, axis=1)
    t_lane = i * qb + lax.broadcasted_iota(jnp.int32, (1, qb), 1)
    sub_pos = lax.broadcasted_iota(jnp.int32, (kc, 1), 0)

    assert qb % kc == 0 and WINDOW % kc == 0 and len(stage) == (qb + WINDOW) // kc
    first = i * (qb // kc) - WINDOW // kc
    dist0 = (lax.broadcasted_iota(jnp.int32, (kc, qb), 1) - lax.broadcasted_iota(jnp.int32, (kc, qb), 0)) + WINDOW
    for step in range(len(stage)):
        c = first + step
        exists = jnp.where(c >= 0, 0.0, MASKED)
        dist = dist0 - step * kc
        bias = jnp.where(dist >= 0, jnp.where(dist < WINDOW, exists, MASKED), MASKED)
        start = pl.multiple_of(jnp.maximum(c, 0) * kc, kc)
        stage[step][...] = _dot_nt(kw_ref[0, pl.ds(start, kc), :], q) + heads(bias)
    win = _flash_init(C_DIM, C_HPG * qb)
    for step in range(len(stage)):
        start = pl.multiple_of(jnp.maximum(first + step, 0) * kc, kc)
        win = _flash_update(stage[step][...], vwt_ref[:, pl.ds(start, kc)], win)
    o_win = _flash_result(win, C_DIM)

    n_cmp = kc_ref.shape[2]
    cmp_end = lax.broadcasted_iota(jnp.int32, (n_cmp, 1), 0) * CMP_STRIDE + (CMP_LEN - 1)
    vis = heads(jnp.where(cmp_end <= t_lane, 1.0, 0.0))
    s = _dot_nt(kc_ref[0, 0], q) + (vis - 1.0) * (-MASKED)
    e = jnp.exp2(s - jnp.max(s, axis=0, keepdims=True)) * vis
    den = jnp.sum(e, axis=0, keepdims=True)
    p_cmp = e / jnp.where(den > 0, den, 1.0)
    o_cmp = _dot(vct_ref[0, 0], p_cmp.astype(BF16))

    p_sum = p_cmp[:, 0:qb]
    for hd in range(1, C_HPG):
        p_sum = p_sum + p_cmp[:, qb * hd:qb * (hd + 1)]
    blk_id = lax.broadcasted_iota(jnp.int32, (n_blk, n_cmp), 0)
    cmp_id = lax.broadcasted_iota(jnp.int32, (n_blk, n_cmp), 1)
    ratio = SEL_BLOCK // CMP_STRIDE
    overlap_t = ((cmp_id < ratio * (blk_id + 1)) & (cmp_id + CMP_LEN // CMP_STRIDE > ratio * blk_id))
    overlap_t = jnp.where(overlap_t, 1.0, 0.0).astype(BF16)
    imp = sum(_dot(overlap_t, part) for part in _split3(p_sum))
    j_sub = lax.broadcasted_iota(jnp.int32, (n_blk, 1), 0)
    cur = t_lane // SEL_BLOCK
    forced = (j_sub == 0) | (j_sub == cur) | (j_sub == cur - 1)
    imp = jnp.where(forced, jnp.inf, imp)
    imp = jnp.where(j_sub * SEL_BLOCK <= t_lane, imp, -jnp.inf)
    ranks = []
    for tile in range(n_blk // SUBLANES):
        mine = imp[SUBLANES * tile:SUBLANES * (tile + 1), :]
        j_tile = j_sub[SUBLANES * tile:SUBLANES * (tile + 1), :]
        rank = jnp.zeros((SUBLANES, qb), jnp.int32)
        for other in range(n_blk):
            row = imp[other:other + 1, :]
            ge = jnp.where(row >= mine, 1, 0)
            gt = jnp.where(row > mine, 1, 0)
            if other < SUBLANES * tile:
                rank = rank + ge
            elif other >= SUBLANES * (tile + 1):
                rank = rank + gt
            else:
                rank = rank + jnp.where(j_tile > other, ge, gt)
        ranks.append(rank)
    sel_bias = jnp.where(jnp.concatenate(ranks, axis=0) < top_n, 0.0, MASKED).astype(BF16)

    key_blk = lax.broadcasted_iota(jnp.int32, (kc, n_blk), 0) // SEL_BLOCK
    blk_col = lax.broadcasted_iota(jnp.int32, (kc, n_blk), 1)

    n_slc = (i + 1) * (qb // kc)

    def slc_scores(c):
        cc = jnp.minimum(c, n_slc - 1)
        expand = jnp.where(blk_col == key_blk + cc * (kc // SEL_BLOCK), 1.0, 0.0).astype(BF16)
        block_bias = _dot(expand, sel_bias)
        bias = jnp.where(c * kc + sub_pos <= t_lane, block_bias, MASKED)
        k = ks_ref[0, pl.ds(pl.multiple_of(cc * kc, kc), kc), :]
        return _dot_nt(k, q) + heads(bias)

    def slc_values(c):
        cc = jnp.clip(c, 0, n_slc - 1)
        return (vst_ref[:, pl.ds(pl.multiple_of(cc * kc, kc), kc)],)

    (o_slc,) = _attend_pairs((n_slc + 1) // 2, slc_scores, slc_values, sa_ref, sb_ref, C_DIM, C_HPG * qb)

    gates_t = gt_ref[0].T
    cols = []
    for hd in range(C_HPG):
        g0, g1, g2 = (gates_t[3 * hd + j:3 * hd + j + 1, :] for j in range(3))
        blk = slice(qb * hd, qb * (hd + 1))
        cols.append(g0 * o_cmp[:, blk] + g1 * o_slc[:, blk] + g2 * o_win[:, blk])
    o_ref[...] = jnp.concatenate(cols, axis=0).T.astype(BF16)


def _nsa(q, k_cmp, v_cmp_t, ks, vst, kw, vwt, gates, batch, seq):
    n = ks.shape[1]
    nq = seq // C_QB
    n_blk = seq // SEL_BLOCK
    top_n = min(SEL_TOPN, n_blk)
    nchunk = seq // CMP_STRIDE
    k_spec = pl.BlockSpec((1, seq, C_DIM), lambda b, g, i: (g, b, 0))
    vt_spec = pl.BlockSpec((C_DIM, seq), lambda b, g, i: (g, b))
    return pl.pallas_call(
        functools.partial(_nsa_kernel, n_blk=n_blk, top_n=top_n),
        grid=(batch, C_GROUPS, nq),
        in_specs=[
            pl.BlockSpec((C_HPG, C_QB, C_DIM), lambda b, g, i: (g, b * nq + i, 0)),
            pl.BlockSpec((1, 1, nchunk, C_DIM), lambda b, g, i: (g, b, 0, 0)),
            pl.BlockSpec((1, 1, C_DIM, nchunk), lambda b, g, i: (g, b, 0, 0)),
            k_spec, vt_spec, k_spec, vt_spec,
            pl.BlockSpec((1, C_QB, LANES), lambda b, g, i: (g, b * nq + i, 0)),
        ],
        out_specs=pl.BlockSpec((C_QB, C_HPG * C_DIM), lambda b, g, i: (b * nq + i, g)),
        out_shape=jax.ShapeDtypeStruct((n, C_HEADS * C_DIM), BF16),
        scratch_shapes=[pltpu.VMEM((KEY_CHUNK, C_HPG * C_QB), F32)] * ((C_QB + WINDOW) // KEY_CHUNK),
        compiler_params=_cparams(("parallel", "parallel", "arbitrary")),
        name="nsa",
    )(q, k_cmp, v_cmp_t, ks, vst, kw, vwt, gates)


def _odd_out_kernel(x_ref, o_ref, wo_ref, y_ref):
    y_ref[...] = x_ref[...] + _dot(o_ref[...], wo_ref[...])


def _odd_out(x2, o_c, w_out):
    n = x2.shape[0]
    tm = PROJ_TM
    row = lambda i: (i, 0)
    return pl.pallas_call(
        _odd_out_kernel,
        grid=(n // tm,),
        in_specs=[
            pl.BlockSpec((tm, D_MODEL), row),
            pl.BlockSpec((tm, D_MODEL), row),
            pl.BlockSpec((D_MODEL, D_MODEL), lambda i: (0, 0)),
        ],
        out_specs=pl.BlockSpec((tm, D_MODEL), row),
        out_shape=jax.ShapeDtypeStruct((n, D_MODEL), F32),
        compiler_params=_cparams(("parallel",)),
        name="odd_out",
    )(x2, o_c, w_out)


FFN_CHUNKS = 2
FFN_TM = 512


def _ffn_kernel(x_ref, g_ref, wg_ref, wu_ref, wd_ref, gf_ref, y_ref, h_ref, acc_ref, *, final_norm):
    f = pl.program_id(1)

    @pl.when(f == 0)
    def _():
        h_ref[...] = _rms_rows(x_ref[...], g_ref[...]).astype(BF16)

    h = h_ref[...]
    gate = _dot(h, wg_ref[...])
    act = (gate * jax.nn.sigmoid(gate) * _dot(h, wu_ref[...])).astype(BF16)
    part = _dot(act, wd_ref[...])

    @pl.when(f == 0)
    def _():
        acc_ref[...] = part

    @pl.when(f == FFN_CHUNKS - 1)
    def _():
        y = x_ref[...] + acc_ref[...] + part if FFN_CHUNKS > 1 else x_ref[...] + part
        y_ref[...] = _rms_rows(y, gf_ref[...]) if final_norm else y


def _ffn(x2, gain, wg, wu, wd, g_final, final_norm):
    n = x2.shape[0]
    tm = FFN_TM
    fc = D_FF // FFN_CHUNKS
    assert FFN_CHUNKS in (1, 2) and fc * FFN_CHUNKS == D_FF and fc % LANES == 0 and n % tm == 0
    row = lambda i, f: (i, 0)
    full2 = lambda i, f: (0, 0)
    return pl.pallas_call(
        functools.partial(_ffn_kernel, final_norm=final_norm),
        grid=(n // tm, FFN_CHUNKS),
        in_specs=[
            pl.BlockSpec((tm, D_MODEL), row),
            pl.BlockSpec((1, D_MODEL), full2),
            pl.BlockSpec((D_MODEL, fc), lambda i, f: (0, f)),
            pl.BlockSpec((D_MODEL, fc), lambda i, f: (0, f)),
            pl.BlockSpec((fc, D_MODEL), lambda i, f: (f, 0)),
            pl.BlockSpec((1, D_MODEL), full2),
        ],
        out_specs=pl.BlockSpec((tm, D_MODEL), row),
        out_shape=jax.ShapeDtypeStruct((n, D_MODEL), F32),
        scratch_shapes=[pltpu.VMEM((tm, D_MODEL), BF16), pltpu.VMEM((tm, D_MODEL), F32)],
        compiler_params=_cparams(("parallel", "arbitrary")),
        name="ffn",
    )(x2, gain, wg, wu, wd, g_final)


def kernel(x, norm_mix, norm_ffn, norm_final, ev_w_in, ev_kv_gain, ev_w_uk, ev_w_uv, ev_lambda, ev_subln, ev_w_out,
           od_w_in, od_cmp_pe, od_cmp_w1, od_cmp_w2, od_w_out, ffn_w_gate, ffn_w_up, ffn_w_down):
    batch, seq, d = x.shape
    depth = norm_mix.shape[0]
    x2 = x.reshape(batch * seq, d)
    t64 = _rope_tables(seq, 64)
    t32 = _rope_tables(seq, A_ROPE)
    g_final = norm_final.reshape(1, d)
    for layer in range(depth):
        j = layer // 2
        gain = norm_mix[layer].reshape(1, d)
        if layer % 2 == 0:
            qc, kc, cvt, iq, ik, iwt, qb, kb, vbt = _even_proj(
                x2, gain, _even_weight(ev_w_in[j]), _block_diag(ev_w_uk[j]).astype(BF16),
                ev_kv_gain[j].reshape(1, A_KV_RANK), t64, t32, seq)
            o_lat = _dsa(iq, ik, iwt, qc, kc, cvt, batch, seq)
            lam_init = 0.8 - 0.6 * math.exp(-0.3 * layer)
            o_b = _diff(qb, kb, vbt, ev_lambda[j], ev_subln[j].reshape(B_V_DIM, 1), lam_init, batch, seq)
            x2 = _even_out(x2, o_lat, o_b, _block_diag(ev_w_uv[j]).astype(BF16), ev_w_out[j].astype(BF16))
        else:
            q, kc_raw, ks, kw, vc_raw, vst, vwt, gates = _odd_proj(x2, gain, _odd_weight(od_w_in[j]), t64, seq)
            k_cmp, v_cmp_t = _compress(kc_raw, vc_raw, od_cmp_pe[j], od_cmp_w1[j], od_cmp_w2[j], batch, seq)
            o_c = _nsa(q, k_cmp, v_cmp_t, ks, vst, kw, vwt, gates, batch, seq)
            x2 = _odd_out(x2, o_c, od_w_out[j].astype(BF16))
        x2 = _ffn(x2, norm_ffn[layer].reshape(1, d), ffn_w_gate[layer].astype(BF16), ffn_w_up[layer].astype(BF16),
                  ffn_w_down[layer].astype(BF16), g_final, layer == depth - 1)
    return x2.reshape(batch, seq, d)
```

```python
import functools
import math

import numpy as np
import jax
import jax.numpy as jnp
from jax import lax
from jax.experimental import pallas as pl
from jax.experimental.pallas import tpu as pltpu

F32 = jnp.float32
BF16 = jnp.bfloat16

D_MODEL = 1024
ROPE_THETA = 10000.0
NORM_EPS = 1e-6

A_HEADS = 8
A_NOPE = 64
A_ROPE = 32
A_KV_RANK = 128
A_V_DIM = 64
A_SCALE = (A_NOPE + A_ROPE) ** -0.5
IDX_HEADS = 8
IDX_DIM = 64
TOPK_MAX = 256

B_HEADS = 4
B_QK_DIM = 64
B_V_DIM = 2 * B_QK_DIM

C_HEADS = 16
C_GROUPS = 4
C_HPG = C_HEADS // C_GROUPS
C_DIM = 64
CMP_LEN = 32
CMP_STRIDE = 16
CMP_HIDDEN = 128
SEL_BLOCK = 64
SEL_TOPN = 16
WINDOW = 512

D_FF = -(-8 * D_MODEL // (3 * 256)) * 256

EVEN_SPLITS = [A_HEADS * A_NOPE, A_HEADS * A_ROPE, A_KV_RANK, A_ROPE, IDX_HEADS * IDX_DIM, IDX_DIM, IDX_HEADS,
               B_HEADS * 2 * B_QK_DIM, B_HEADS * 2 * B_QK_DIM, B_HEADS * B_V_DIM]
ODD_SPLITS = [C_HEADS * C_DIM] + [C_GROUPS * C_DIM] * 6 + [C_HEADS * 3]

LANES = 128
SUBLANES = 8
VMEM_LIMIT = 56 * 1024 * 1024
MASKED = -1e30
KEY_NEG_INF = -2139095041
HALF16 = 1 << 15
LOG2E = math.log2(math.e)

PROJ_TM = 512
A_QB = 512
KEY_CHUNK = 256
B_QB = 512
C_QB = 256


def _cparams(sem):
    return pltpu.CompilerParams(dimension_semantics=sem, vmem_limit_bytes=VMEM_LIMIT)


def _dot(a, b):
    return jnp.dot(a, b, preferred_element_type=F32)


def _dot_nt(a, b):
    return lax.dot_general(a, b, (((1,), (1,)), ((), ())), preferred_element_type=F32)


def _rms_rows(x, gain):
    return x * lax.rsqrt(jnp.mean(x * x, axis=-1, keepdims=True) + NORM_EPS) * gain


def _rope_block(y, cos, s_hi, s_lo, half):
    return y * cos + pltpu.roll(y, half, 1) * s_hi + pltpu.roll(y, LANES - half, 1) * s_lo


def _rope_tables(seq, d):
    half = d // 2
    pos = jnp.arange(seq, dtype=jnp.int32)
    inv = ROPE_THETA ** (-jnp.arange(0, d, 2, dtype=F32) / d)
    ang = pos.astype(F32)[:, None] * inv[None, :]
    cos, sin = jnp.cos(ang), jnp.sin(ang)
    lane = np.arange(LANES)
    idx = (lane % d) % half
    first = jnp.asarray((lane % d) < half)
    c = cos[:, idx]
    s = sin[:, idx]
    return jnp.stack([c, jnp.where(first[None, :], 0.0, s), jnp.where(first[None, :], -s, 0.0)])


ONES_ROWS = 16


def _flash_probs(s, m):
    m_new = jnp.maximum(m, jnp.max(s, axis=0, keepdims=True))
    return m_new, jnp.exp2(m - m_new), jnp.exp2(s - m_new).astype(BF16)


def _flash_accumulate(acc, alpha, vt, p):
    vt_ones = jnp.concatenate([vt, jnp.ones((ONES_ROWS, vt.shape[1]), BF16)], axis=0)
    return alpha * acc + _dot(vt_ones, p)


def _flash_update(s, vt, state):
    m, acc = state
    m_new, alpha, p = _flash_probs(s, m)
    return m_new, _flash_accumulate(acc, alpha, vt, p)


def _attend_pairs(n_pairs, scores, values, sa_ref, sb_ref, dv, group_lanes, acc_ref=None):
    groups = sa_ref.shape[1] // group_lanes
    if acc_ref is not None:
        assert groups == 1
        acc_ref[...] = jnp.zeros(acc_ref.shape, F32)

        def update_in_place(s_ref, c, m):
            m_new, alpha, p = _flash_probs(s_ref[...], m)
            acc_ref[...] = _flash_accumulate(acc_ref[...], alpha, values(c)[0], p)
            return m_new

        sa_ref[...] = scores(0)

        def trip_in_place(pair, m):
            sb_ref[...] = scores(2 * pair + 1)
            m = update_in_place(sa_ref, 2 * pair, m)
            sa_ref[...] = scores(2 * pair + 2)
            return update_in_place(sb_ref, 2 * pair + 1, m)

        lax.fori_loop(0, n_pairs, trip_in_place, jnp.full((1, group_lanes), MASKED, F32))
        acc = acc_ref[...]
        return (acc[0:dv] / acc[dv:dv + 1],)

    def update(s_ref, c, states):
        vts = values(c)
        return tuple(_flash_update(s_ref[:, group_lanes * g:group_lanes * (g + 1)], vts[g], states[g])
                     for g in range(groups))

    sa_ref[...] = scores(0)

    def trip(pair, states):
        sb_ref[...] = scores(2 * pair + 1)
        states = update(sa_ref, 2 * pair, states)
        sa_ref[...] = scores(2 * pair + 2)
        return update(sb_ref, 2 * pair + 1, states)

    states = lax.fori_loop(0, n_pairs, trip, tuple(_flash_init(dv, group_lanes) for _ in range(groups)))
    return tuple(_flash_result(st, dv) for st in states)


def _flash_init(dv, lanes):
    return (jnp.full((1, lanes), MASKED, F32), jnp.zeros((dv + ONES_ROWS, lanes), F32))


def _flash_result(state, dv):
    _, acc = state
    return acc[0:dv] / acc[dv:dv + 1]


EV_QN = (0, 512)
EV_R64 = (512, 2176)
EV_R32 = (2176, 3328)
EV_CKV = (3328, 3456)
EV_VB = (3456, 3968)
EV_IW = (3968, 4096)
EV_COLS = 4096


def _even_weight(w_in):
    qa_nope, qa_rope, c_kv, ka_rope, iq, ik, iw, qb, kb, vb = jnp.split(w_in, np.cumsum(EVEN_SPLITS)[:-1].tolist(), axis=1)
    d = w_in.shape[0]
    z = lambda n: jnp.zeros((d, n), w_in.dtype)
    qr = jnp.pad(qa_rope.reshape(d, A_HEADS, A_ROPE), ((0, 0), (0, 0), (0, LANES - A_ROPE))).reshape(d, A_HEADS * LANES)
    cols = [qa_nope, iq, qb, kb, ik, z(LANES - IDX_DIM), qr, ka_rope, z(LANES - A_ROPE), c_kv, vb, iw, z(LANES - IDX_HEADS)]
    w = jnp.concatenate(cols, axis=1)
    assert w.shape[1] == EV_COLS
    return w.astype(BF16)


def _block_diag(w):
    h, a, b = w.shape
    eye = jnp.eye(h, dtype=w.dtype)
    return (eye[:, None, :, None] * w[:, :, None, :]).reshape(h * a, h * b)


def _even_proj_kernel(x_ref, g_ref, w_ref, wuk_ref, kvg_ref, t64_ref, t32_ref,
                      qc_ref, kc_ref, cvt_ref, iq_ref, ik_ref, iwt_ref, qb_ref, kb_ref, vbt_ref):
    h = _rms_rows(x_ref[...], g_ref[...]).astype(BF16)

    def proj(seg):
        return _dot(h, w_ref[:, seg[0]:seg[1]])

    q_lat = _dot(proj(EV_QN).astype(BF16), wuk_ref[...]) * (A_SCALE * LOG2E)
    for hd in range(A_HEADS):
        qc_ref[hd, :, 0:LANES] = q_lat[:, LANES * hd:LANES * (hd + 1)].astype(BF16)

    cos, s_hi, s_lo = t64_ref[0], t64_ref[1], t64_ref[2]
    y = proj(EV_R64)
    half = LANES // 2
    for blk in range((EV_R64[1] - EV_R64[0]) // LANES):
        r = _rope_block(y[:, LANES * blk:LANES * (blk + 1)], cos, s_hi, s_lo, IDX_DIM // 2)
        if blk < 4:
            r = (r * IDX_DIM ** -0.5).astype(BF16)
            iq_ref[2 * blk] = r[:, :half]
            iq_ref[2 * blk + 1] = r[:, half:]
        elif blk < 8:
            r = (r * (B_QK_DIM ** -0.5 * LOG2E)).astype(BF16)
            qb_ref[2 * (blk - 4)] = r[:, :half]
            qb_ref[2 * (blk - 4) + 1] = r[:, half:]
        elif blk < 12:
            r = r.astype(BF16)
            kb_ref[2 * (blk - 8)] = r[:, :half]
            kb_ref[2 * (blk - 8) + 1] = r[:, half:]
        else:
            ik_ref[...] = r[:, :half].astype(BF16)

    cos, s_hi, s_lo = t32_ref[0], t32_ref[1], t32_ref[2]
    y = proj(EV_R32)
    for blk in range(A_HEADS + 1):
        r = _rope_block(y[:, LANES * blk:LANES * (blk + 1)], cos, s_hi, s_lo, A_ROPE // 2)
        if blk < A_HEADS:
            qc_ref[blk, :, LANES:2 * LANES] = (r * (A_SCALE * LOG2E)).astype(BF16)
        else:
            kc_ref[:, LANES:2 * LANES] = r.astype(BF16)

    c_kv = _rms_rows(proj(EV_CKV), kvg_ref[...])
    kc_ref[:, 0:LANES] = c_kv.astype(BF16)
    cvt_ref[...] = c_kv.T.astype(BF16)
    vbt_ref[...] = proj(EV_VB).T.astype(BF16)
    iwt_ref[...] = (proj(EV_IW) * IDX_HEADS ** -0.5).T[0:IDX_HEADS, :]


def _even_proj(x2, gain, w, wuk_bd, kv_gain, t64, t32, seq):
    n = x2.shape[0]
    tm = PROJ_TM
    tiles_per_seq = seq // tm
    row = lambda i: (i, 0)
    col = lambda i: (0, i)
    hrow = lambda i: (0, i, 0)
    full2 = lambda i: (0, 0)
    tab = lambda i: (0, i % tiles_per_seq, 0)
    out_shape = (
        jax.ShapeDtypeStruct((A_HEADS, n, 2 * LANES), BF16),
        jax.ShapeDtypeStruct((n, 2 * LANES), BF16),
        jax.ShapeDtypeStruct((A_KV_RANK, n), BF16),
        jax.ShapeDtypeStruct((IDX_HEADS, n, IDX_DIM), BF16),
        jax.ShapeDtypeStruct((n, IDX_DIM), BF16),
        jax.ShapeDtypeStruct((IDX_HEADS, n), F32),
        jax.ShapeDtypeStruct((2 * B_HEADS, n, B_QK_DIM), BF16),
        jax.ShapeDtypeStruct((2 * B_HEADS, n, B_QK_DIM), BF16),
        jax.ShapeDtypeStruct((B_HEADS * B_V_DIM, n), BF16),
    )
    return pl.pallas_call(
        _even_proj_kernel,
        grid=(n // tm,),
        in_specs=[
            pl.BlockSpec((tm, D_MODEL), row),
            pl.BlockSpec((1, D_MODEL), full2),
            pl.BlockSpec((D_MODEL, EV_COLS), full2),
            pl.BlockSpec((A_HEADS * A_NOPE, A_HEADS * A_KV_RANK), full2),
            pl.BlockSpec((1, A_KV_RANK), full2),
            pl.BlockSpec((3, tm, LANES), tab),
            pl.BlockSpec((3, tm, LANES), tab),
        ],
        out_specs=(
            pl.BlockSpec((A_HEADS, tm, 2 * LANES), hrow),
            pl.BlockSpec((tm, 2 * LANES), row),
            pl.BlockSpec((A_KV_RANK, tm), col),
            pl.BlockSpec((IDX_HEADS, tm, IDX_DIM), hrow),
            pl.BlockSpec((tm, IDX_DIM), row),
            pl.BlockSpec((IDX_HEADS, tm), col),
            pl.BlockSpec((2 * B_HEADS, tm, B_QK_DIM), hrow),
            pl.BlockSpec((2 * B_HEADS, tm, B_QK_DIM), hrow),
            pl.BlockSpec((B_HEADS * B_V_DIM, tm), col),
        ),
        out_shape=out_shape,
        compiler_params=_cparams(("parallel",)),
        name="even_proj",
    )(x2, gain, w, wuk_bd, kv_gain, t64, t32)


def _dsa_kernel(iq_ref, ik_ref, iwt_ref, qc_ref, kc_ref, cvt_ref, o_ref,
                keys_ref, hi_ref, lo_ref, lo2_ref, jstar_ref, sa_ref, sb_ref, acc_ref, *, k_sel, seq):
    i = pl.program_id(1)
    qb, kc = A_QB, KEY_CHUNK
    n_chunks = (i * qb + qb + kc - 1) // kc
    t_lane = i * qb + lax.broadcasted_iota(jnp.int32, (1, qb), 1)
    sub_pos = lax.broadcasted_iota(jnp.int32, (kc, 1), 0)
    iwt = iwt_ref[...]

    iq = iq_ref[...].reshape(IDX_HEADS * qb, IDX_DIM)

    def index_chunk(c):
        k = ik_ref[pl.ds(pl.multiple_of(c * kc, kc), kc), :]
        rel_all = jnp.maximum(_dot_nt(k, iq), 0.0)
        score = None
        for hd in range(IDX_HEADS):
            rel = rel_all[:, qb * hd:qb * (hd + 1)] * iwt[hd:hd + 1, :]
            score = rel if score is None else score + rel
        bits = lax.bitcast_convert_type(score, jnp.int32)
        key = jnp.where(bits < 0, bits ^ 0x7FFFFFFF, bits)
        key = jnp.where(score == 0.0, 0, key)
        key = jnp.where(c * kc + sub_pos <= t_lane, key, KEY_NEG_INF)
        keys_ref[c] = key
        hi_ref[c] = (key >> 16).astype(jnp.int16)
        lo_ref[c] = ((key & 0xFFFF) - HALF16).astype(jnp.int16)

    def index_pair(pair, carry):
        index_chunk(2 * pair)
        index_chunk(jnp.minimum(2 * pair + 1, n_chunks - 1))
        return carry

    n_pairs = (n_chunks + 1) // 2
    lax.fori_loop(0, n_pairs, index_pair, 0)

    pad_chunk = hi_ref.shape[0] - 1
    hi_ref[pad_chunk] = jnp.full((kc, qb), -HALF16, jnp.int16)
    lo2_ref[pad_chunk] = jnp.full((kc, qb), -HALF16, jnp.int16)
    rows16 = 2 * SUBLANES

    one16, zero16 = jnp.int16(1), jnp.int16(0)

    def paired_sum16(hits):
        def fold(hit):
            parts = [hit[rows16 * r:rows16 * (r + 1), :] for r in range(kc // rows16)]
            while len(parts) > 1:
                parts = [a + b for a, b in zip(parts[0::2], parts[1::2])]
            return parts[0]

        def body(pair, acc):
            second = jnp.where(2 * pair + 1 < n_chunks, 2 * pair + 1, pad_chunk)
            return acc + fold(hits(2 * pair, 2 * pair)) + fold(hits(second, 2 * pair + 1))
        acc = lax.fori_loop(0, n_pairs, body, jnp.zeros((rows16, qb), jnp.int16))
        return jnp.sum(acc.astype(jnp.int32), axis=0, keepdims=True)

    def count16(ref, pred):
        return paired_sum16(lambda c, _: jnp.where(pred(ref[c]), one16, zero16))

    def kth_largest16(ref, k_row):
        u = jnp.where(count16(ref, lambda x: x >= jnp.int16(0)) >= k_row, 0, -HALF16)

        def bit(it, u):
            cand = u + jnp.left_shift(jnp.int32(1), 14 - it)
            c16 = cand.astype(jnp.int16)
            return jnp.where(count16(ref, lambda x: x >= c16) >= k_row, cand, u)
        return lax.fori_loop(0, 15, bit, u)

    v_hi = kth_largest16(hi_ref, k_sel)
    v_hi16 = v_hi.astype(jnp.int16)
    k_low = k_sel - count16(hi_ref, lambda x: x > v_hi16)

    def mask_low(c, carry):
        lo2_ref[c] = jnp.where(hi_ref[c] == v_hi16, lo_ref[c], jnp.int16(-HALF16))
        return carry

    lax.fori_loop(0, n_chunks, mask_low, 0)
    v_lo = kth_largest16(lo2_ref, k_low)
    v_lo16 = v_lo.astype(jnp.int16)
    v = v_hi * (2 * HALF16) + (v_lo + HALF16)
    count_ge = (k_sel - k_low) + count16(lo2_ref, lambda x: x >= v_lo16)

    jstar_ref[...] = jnp.full(jstar_ref.shape, seq - 1, jnp.int32)

    @pl.when(jnp.max(count_ge) > k_sel)
    def _():
        need = k_low - count16(lo2_ref, lambda x: x > v_lo16)

        def mark_tied(c, carry):
            lo2_ref[c] = jnp.where(hi_ref[c] == v_hi16, jnp.where(lo_ref[c] == v_lo16, one16, zero16), zero16)
            return carry

        lax.fori_loop(0, n_chunks, mark_tied, 0)
        lo2_ref[pad_chunk] = jnp.zeros((kc, qb), jnp.int16)
        pos16 = lax.broadcasted_iota(jnp.int32, (kc, qb), 0).astype(jnp.int16)

        def tied_below(cand):
            def hits(c, c_pos):
                local = jnp.clip(cand - c_pos * kc, 0, kc).astype(jnp.int16)
                return jnp.where(pos16 < local, lo2_ref[c], zero16)
            return paired_sum16(hits)

        def index_bit(it, x):
            cand = x + jnp.left_shift(jnp.int32(1), (seq.bit_length() - 2) - it)
            return jnp.where(tied_below(cand) < need, cand, x)

        x = lax.fori_loop(0, seq.bit_length() - 1, index_bit, jnp.zeros((1, qb), jnp.int32))
        jstar_ref[...] = jnp.broadcast_to(x, jstar_ref.shape)

    j_star = jstar_ref[0:1, :]

    q = qc_ref[...].reshape(A_HEADS * qb, 2 * LANES)

    def scores(c):
        cc = jnp.minimum(c, n_chunks - 1)
        kv = kc_ref[pl.ds(pl.multiple_of(cc * kc, kc), kc), :]
        pos = c * kc + sub_pos
        key = keys_ref[cc]
        key = jnp.where(pos > j_star, key - 1, key)
        bias = jnp.where(key >= v, jnp.where(pos <= t_lane, 0.0, MASKED), MASKED)
        return _dot_nt(kv, q) + jnp.concatenate([bias] * A_HEADS, axis=1)

    def values(c):
        cc = jnp.clip(c, 0, n_chunks - 1)
        return (cvt_ref[:, pl.ds(pl.multiple_of(cc * kc, kc), kc)],)

    (o,) = _attend_pairs(n_pairs, scores, values, sa_ref, sb_ref, A_KV_RANK, A_HEADS * qb, acc_ref)
    for hd in range(A_HEADS):
        o_ref[:, A_KV_RANK * hd:A_KV_RANK * (hd + 1)] = o[:, qb * hd:qb * (hd + 1)].T.astype(BF16)


def _dsa(iq, ik, iwt, qc, kc, cvt, batch, seq):
    n = ik.shape[0]
    nq = seq // A_QB
    k_sel = min(TOPK_MAX, seq // 4)
    assert seq & (seq - 1) == 0 and seq % KEY_CHUNK == 0 and k_sel <= KEY_CHUNK and A_QB % LANES == 0
    qrow = lambda b, i: (b * nq + i, 0)
    hqrow = lambda b, i: (0, b * nq + i, 0)
    brow = lambda b, i: (b, 0)
    return pl.pallas_call(
        functools.partial(_dsa_kernel, k_sel=k_sel, seq=seq),
        grid=(batch, nq),
        in_specs=[
            pl.BlockSpec((IDX_HEADS, A_QB, IDX_DIM), hqrow),
            pl.BlockSpec((seq, IDX_DIM), brow),
            pl.BlockSpec((IDX_HEADS, A_QB), lambda b, i: (0, b * nq + i)),
            pl.BlockSpec((A_HEADS, A_QB, 2 * LANES), hqrow),
            pl.BlockSpec((seq, 2 * LANES), brow),
            pl.BlockSpec((A_KV_RANK, seq), lambda b, i: (0, b)),
        ],
        out_specs=pl.BlockSpec((A_QB, A_HEADS * A_KV_RANK), qrow),
        out_shape=jax.ShapeDtypeStruct((n, A_HEADS * A_KV_RANK), BF16),
        scratch_shapes=[pltpu.VMEM((seq // KEY_CHUNK, KEY_CHUNK, A_QB), jnp.int32),
                        pltpu.VMEM((seq // KEY_CHUNK + 1, KEY_CHUNK, A_QB), jnp.int16),
                        pltpu.VMEM((seq // KEY_CHUNK, KEY_CHUNK, A_QB), jnp.int16),
                        pltpu.VMEM((seq // KEY_CHUNK + 1, KEY_CHUNK, A_QB), jnp.int16),
                        pltpu.VMEM((SUBLANES, A_QB), jnp.int32),
                        pltpu.VMEM((KEY_CHUNK, A_HEADS * A_QB), F32),
                        pltpu.VMEM((KEY_CHUNK, A_HEADS * A_QB), F32),
                        pltpu.VMEM((A_KV_RANK + ONES_ROWS, A_HEADS * A_QB), F32)],
        compiler_params=_cparams(("parallel", "arbitrary")),
        name="dsa",
    )(iq, ik, iwt, qc, kc, cvt)


def _diff_kernel(q_ref, k_ref, vt_ref, lam_ref, g_ref, o_ref, sa_ref, sb_ref, *, lam_init):
    i = pl.program_id(1)
    qb, kc = B_QB, KEY_CHUNK
    nsub = 2 * B_HEADS
    n_chunks = (i + 1) * (qb // kc)
    key_pos = lax.broadcasted_iota(jnp.int32, (kc, qb), 0)
    t_lane = i * qb + lax.broadcasted_iota(jnp.int32, (kc, qb), 1)

    def scores(c):
        start = pl.multiple_of(jnp.minimum(c, n_chunks - 1) * kc, kc)
        bias = jnp.where(c * kc + key_pos <= t_lane, 0.0, MASKED)
        return jnp.concatenate([_dot_nt(k_ref[j, pl.ds(start, kc), :], q_ref[j]) + bias for j in range(nsub)], axis=1)

    def values(c):
        start = pl.multiple_of(jnp.clip(c, 0, n_chunks - 1) * kc, kc)
        return tuple(vt_ref[B_V_DIM * hd:B_V_DIM * (hd + 1), pl.ds(start, kc)] for hd in range(B_HEADS))

    outs = _attend_pairs((n_chunks + 1) // 2, scores, values, sa_ref, sb_ref, B_V_DIM, 2 * qb)

    lam = lam_ref[...]
    lam_full = (jnp.exp(jnp.sum(lam[0:1] * lam[1:2], axis=-1, keepdims=True))
                - jnp.exp(jnp.sum(lam[2:3] * lam[3:4], axis=-1, keepdims=True)) + lam_init)
    gain_col = g_ref[...]
    for hd in range(B_HEADS):
        o = outs[hd]
        d = o[:, 0:qb] - lam_full * o[:, qb:2 * qb]
        d = d * lax.rsqrt(jnp.mean(d * d, axis=0, keepdims=True) + NORM_EPS) * gain_col * (1.0 - lam_init)
        o_ref[:, B_V_DIM * hd:B_V_DIM * (hd + 1)] = d.T.astype(BF16)


def _diff(qb, kb, vbt, lam, subln, lam_init, batch, seq):
    n = vbt.shape[1]
    nq = seq // B_QB
    assert B_QB % KEY_CHUNK == 0
    return pl.pallas_call(
        functools.partial(_diff_kernel, lam_init=lam_init),
        grid=(batch, nq),
        in_specs=[
            pl.BlockSpec((2 * B_HEADS, B_QB, B_QK_DIM), lambda b, i: (0, b * nq + i, 0)),
            pl.BlockSpec((2 * B_HEADS, seq, B_QK_DIM), lambda b, i: (0, b, 0)),
            pl.BlockSpec((B_HEADS * B_V_DIM, seq), lambda b, i: (0, b)),
            pl.BlockSpec((4, B_QK_DIM), lambda b, i: (0, 0)),
            pl.BlockSpec((B_V_DIM, 1), lambda b, i: (0, 0)),
        ],
        out_specs=pl.BlockSpec((B_QB, B_HEADS * B_V_DIM), lambda b, i: (b * nq + i, 0)),
        out_shape=jax.ShapeDtypeStruct((n, B_HEADS * B_V_DIM), BF16),
        scratch_shapes=[pltpu.VMEM((KEY_CHUNK, 2 * B_HEADS * B_QB), F32)] * 2,
        compiler_params=_cparams(("parallel", "arbitrary")),
        name="diff_attn",
    )(qb, kb, vbt, lam, subln)


def _even_out_kernel(x_ref, ol_ref, ob_ref, wuv_ref, wo_ref, y_ref):
    half = A_HEADS * A_V_DIM
    o_a = _dot(ol_ref[...], wuv_ref[...]).astype(BF16)
    y_ref[...] = x_ref[...] + _dot(o_a, wo_ref[0:half, :]) + _dot(ob_ref[...], wo_ref[half:, :])


def _even_out(x2, o_lat, o_b, wuv_bd, w_out):
    n = x2.shape[0]
    tm = PROJ_TM
    row = lambda i: (i, 0)
    full2 = lambda i: (0, 0)
    return pl.pallas_call(
        _even_out_kernel,
        grid=(n // tm,),
        in_specs=[
            pl.BlockSpec((tm, D_MODEL), row),
            pl.BlockSpec((tm, A_HEADS * A_KV_RANK), row),
            pl.BlockSpec((tm, B_HEADS * B_V_DIM), row),
            pl.BlockSpec((A_HEADS * A_KV_RANK, A_HEADS * A_V_DIM), full2),
            pl.BlockSpec((D_MODEL, D_MODEL), full2),
        ],
        out_specs=pl.BlockSpec((tm, D_MODEL), row),
        out_shape=jax.ShapeDtypeStruct((n, D_MODEL), F32),
        compiler_params=_cparams(("parallel",)),
        name="even_out",
    )(x2, o_lat, o_b, wuv_bd, w_out)


OD_R64 = (0, 1792)
OD_V = (1792, 2560)
OD_G = (2560, 3072)
OD_COLS = 3072
GATES_PER_GROUP = C_HPG * 3
KV_WIDTH = C_GROUPS * C_DIM


def _odd_weight(w_in):
    qc, kc, vc, ks, vs, kw, vw, gc = jnp.split(w_in, np.cumsum(ODD_SPLITS)[:-1].tolist(), axis=1)
    d = w_in.shape[0]
    gates = jnp.pad(gc.reshape(d, C_GROUPS, GATES_PER_GROUP), ((0, 0), (0, 0), (0, LANES - GATES_PER_GROUP)))
    w = jnp.concatenate([qc, kc, ks, kw, vc, vs, vw, gates.reshape(d, C_GROUPS * LANES)], axis=1)
    assert w.shape[1] == OD_COLS
    return w.astype(BF16)


def _odd_proj_kernel(x_ref, g_ref, w_ref, t64_ref, q_ref, kc_ref, ks_ref, kw_ref, vc_ref, vst_ref, vwt_ref, gt_ref):
    h = _rms_rows(x_ref[...], g_ref[...]).astype(BF16)
    half = LANES // 2
    cos, s_hi, s_lo = t64_ref[0], t64_ref[1], t64_ref[2]
    y = _dot(h, w_ref[:, OD_R64[0]:OD_R64[1]])
    for blk in range((OD_R64[1] - OD_R64[0]) // LANES):
        r = _rope_block(y[:, LANES * blk:LANES * (blk + 1)], cos, s_hi, s_lo, C_DIM // 2)
        if blk < 8:
            r = (r * (C_DIM ** -0.5 * LOG2E)).astype(BF16)
            dst, j = q_ref, blk
        else:
            r = r.astype(BF16)
            dst, j = (kc_ref, ks_ref, kw_ref)[(blk - 8) // 2], (blk - 8) % 2
        dst[2 * j] = r[:, :half]
        dst[2 * j + 1] = r[:, half:]
    y = _dot(h, w_ref[:, OD_V[0]:OD_V[1]])
    vc = y[:, 0:KV_WIDTH].astype(BF16)
    for g in range(C_GROUPS):
        vc_ref[g] = vc[:, C_DIM * g:C_DIM * (g + 1)]
    vst_ref[...] = y[:, KV_WIDTH:2 * KV_WIDTH].T.astype(BF16)
    vwt_ref[...] = y[:, 2 * KV_WIDTH:3 * KV_WIDTH].T.astype(BF16)
    y = jax.nn.sigmoid(_dot(h, w_ref[:, OD_G[0]:OD_G[1]]))
    for g in range(C_GROUPS):
        gt_ref[g] = y[:, LANES * g:LANES * (g + 1)]


def _odd_proj(x2, gain, w, t64, seq):
    n = x2.shape[0]
    tm = PROJ_TM
    tiles_per_seq = seq // tm
    row = lambda i: (i, 0)
    col = lambda i: (0, i)
    hrow = lambda i: (0, i, 0)
    full2 = lambda i: (0, 0)
    kv_shape = jax.ShapeDtypeStruct((C_GROUPS, n, C_DIM), BF16)
    kv_spec = pl.BlockSpec((C_GROUPS, tm, C_DIM), hrow)
    kvt_shape = jax.ShapeDtypeStruct((KV_WIDTH, n), BF16)
    kvt_spec = pl.BlockSpec((KV_WIDTH, tm), col)
    return pl.pallas_call(
        _odd_proj_kernel,
        grid=(n // tm,),
        in_specs=[
            pl.BlockSpec((tm, D_MODEL), row),
            pl.BlockSpec((1, D_MODEL), full2),
            pl.BlockSpec((D_MODEL, OD_COLS), full2),
            pl.BlockSpec((3, tm, LANES), lambda i: (0, i % tiles_per_seq, 0)),
        ],
        out_specs=(pl.BlockSpec((C_HEADS, tm, C_DIM), hrow),) + (kv_spec,) * 4 + (kvt_spec,) * 2
        + (pl.BlockSpec((C_GROUPS, tm, LANES), hrow),),
        out_shape=(jax.ShapeDtypeStruct((C_HEADS, n, C_DIM), BF16),) + (kv_shape,) * 4 + (kvt_shape,) * 2
        + (jax.ShapeDtypeStruct((C_GROUPS, n, LANES), F32),),
        compiler_params=_cparams(("parallel",)),
        name="odd_proj",
    )(x2, gain, w, t64)


def _gelu_tanh(x):
    return 0.5 * x * (1.0 + jnp.tanh(math.sqrt(2.0 / math.pi) * (x + 0.044715 * (x * x * x))))


def _compress_kernel(kch_ref, vch_ref, pe_ref, w1_ref, w2_ref, w2t_ref, kc_ref, vct_ref):
    rows = kch_ref.shape[2]

    def hidden(src, kv):
        ch = src[0, 0].astype(F32)
        first = _dot((ch + pe_ref[kv, 0:1, :]).astype(BF16), w1_ref[kv, 0])
        second = _dot((ch + pe_ref[kv, 1:2, :]).astype(BF16), w1_ref[kv, 1])
        return _gelu_tanh(first + pltpu.roll(second, rows - 1, 0)).astype(BF16)

    kc_ref[0, 0] = _dot(hidden(kch_ref, 0), w2_ref[...]).astype(BF16)
    vct_ref[0, 0] = _dot_nt(w2t_ref[...], hidden(vch_ref, 1)).astype(BF16)


def _compress(kc_raw, vc_raw, pe, w1, w2, batch, seq):
    nchunk = seq // CMP_STRIDE
    width = CMP_STRIDE * C_DIM
    kch = kc_raw.reshape(C_GROUPS, batch, nchunk, width)
    vch = vc_raw.reshape(C_GROUPS, batch, nchunk, width)
    pe2 = pe.reshape(2, 2, width)
    w1s = w1.reshape(2, 2, width, CMP_HIDDEN).astype(BF16)
    blk = lambda g, b: (g, b, 0, 0)
    full2 = lambda g, b: (0, 0)
    return pl.pallas_call(
        _compress_kernel,
        grid=(C_GROUPS, batch),
        in_specs=[
            pl.BlockSpec((1, 1, nchunk, width), blk),
            pl.BlockSpec((1, 1, nchunk, width), blk),
            pl.BlockSpec((2, 2, width), lambda g, b: (0, 0, 0)),
            pl.BlockSpec((2, 2, width, CMP_HIDDEN), lambda g, b: (0, 0, 0, 0)),
            pl.BlockSpec((CMP_HIDDEN, C_DIM), full2),
            pl.BlockSpec((C_DIM, CMP_HIDDEN), full2),
        ],
        out_specs=(pl.BlockSpec((1, 1, nchunk, C_DIM), blk), pl.BlockSpec((1, 1, C_DIM, nchunk), blk)),
        out_shape=(jax.ShapeDtypeStruct((C_GROUPS, batch, nchunk, C_DIM), BF16),
                   jax.ShapeDtypeStruct((C_GROUPS, batch, C_DIM, nchunk), BF16)),
        compiler_params=_cparams(("parallel", "parallel")),
        name="nsa_compress",
    )(kch, vch, pe2, w1s, w2[0].astype(BF16), w2[1].T.astype(BF16))


def _split3(x):
    a = x.astype(BF16)
    r = x - a.astype(F32)
    b = r.astype(BF16)
    c = (r - b.astype(F32)).astype(BF16)
    return a, b, c


def _nsa_kernel(q_ref, kc_ref, vct_ref, ks_ref, vst_ref, kw_ref, vwt_ref, gt_ref, o_ref, *stage, n_blk, top_n):
    sa_ref, sb_ref = stage[0], stage[1]
    i = pl.program_id(2)
    qb, kc = C_QB, KEY_CHUNK
    q = q_ref[...].reshape(C_HPG * qb, C_DIM)
    heads = lambda a: jnp.concatenate([a] * C_HPG, axis=1)
    t_lane = i * qb + lax.broadcasted_iota(jnp.int32, (1, qb), 1)
    sub_pos = lax.broadcasted_iota(jnp.int32, (kc, 1), 0)

    assert qb % kc == 0 and WINDOW % kc == 0 and len(stage) == (qb + WINDOW) // kc
    first = i * (qb // kc) - WINDOW // kc
    dist0 = (lax.broadcasted_iota(jnp.int32, (kc, qb), 1) - lax.broadcasted_iota(jnp.int32, (kc, qb), 0)) + WINDOW
    for step in range(len(stage)):
        c = first + step
        exists = jnp.where(c >= 0, 0.0, MASKED)
        dist = dist0 - step * kc
        bias = jnp.where(dist >= 0, jnp.where(dist < WINDOW, exists, MASKED), MASKED)
        start = pl.multiple_of(jnp.maximum(c, 0) * kc, kc)
        stage[step][...] = _dot_nt(kw_ref[0, pl.ds(start, kc), :], q) + heads(bias)
    win = _flash_init(C_DIM, C_HPG * qb)
    for step in range(len(stage)):
        start = pl.multiple_of(jnp.maximum(first + step, 0) * kc, kc)
        win = _flash_update(stage[step][...], vwt_ref[:, pl.ds(start, kc)], win)
    o_win = _flash_result(win, C_DIM)

    n_cmp = kc_ref.shape[2]
    cmp_end = lax.broadcasted_iota(jnp.int32, (n_cmp, 1), 0) * CMP_STRIDE + (CMP_LEN - 1)
    vis = heads(jnp.where(cmp_end <= t_lane, 1.0, 0.0))
    s = _dot_nt(kc_ref[0, 0], q) + (vis - 1.0) * (-MASKED)
    e = jnp.exp2(s - jnp.max(s, axis=0, keepdims=True)) * vis
    den = jnp.sum(e, axis=0, keepdims=True)
    p_cmp = e / jnp.where(den > 0, den, 1.0)
    o_cmp = _dot(vct_ref[0, 0], p_cmp.astype(BF16))

    p_sum = p_cmp[:, 0:qb]
    for hd in range(1, C_HPG):
        p_sum = p_sum + p_cmp[:, qb * hd:qb * (hd + 1)]
    blk_id = lax.broadcasted_iota(jnp.int32, (n_blk, n_cmp), 0)
    cmp_id = lax.broadcasted_iota(jnp.int32, (n_blk, n_cmp), 1)
    ratio = SEL_BLOCK // CMP_STRIDE
    overlap_t = ((cmp_id < ratio * (blk_id + 1)) & (cmp_id + CMP_LEN // CMP_STRIDE > ratio * blk_id))
    overlap_t = jnp.where(overlap_t, 1.0, 0.0).astype(BF16)
    imp = sum(_dot(overlap_t, part) for part in _split3(p_sum))
    j_sub = lax.broadcasted_iota(jnp.int32, (n_blk, 1), 0)
    cur = t_lane // SEL_BLOCK
    forced = (j_sub == 0) | (j_sub == cur) | (j_sub == cur - 1)
    imp = jnp.where(forced, jnp.inf, imp)
    imp = jnp.where(j_sub * SEL_BLOCK <= t_lane, imp, -jnp.inf)
    ranks = []
    for tile in range(n_blk // SUBLANES):
        mine = imp[SUBLANES * tile:SUBLANES * (tile + 1), :]
        j_tile = j_sub[SUBLANES * tile:SUBLANES * (tile + 1), :]
        rank = jnp.zeros((SUBLANES, qb), jnp.int32)
        for other in range(n_blk):
            row = imp[other:other + 1, :]
            ge = jnp.where(row >= mine, 1, 0)
            gt = jnp.where(row > mine, 1, 0)
            if other < SUBLANES * tile:
                rank = rank + ge
            elif other >= SUBLANES * (tile + 1):
                rank = rank + gt
            else:
                rank = rank + jnp.where(j_tile > other, ge, gt)
        ranks.append(rank)
    sel_bias = jnp.where(jnp.concatenate(ranks, axis=0) < top_n, 0.0, MASKED).astype(BF16)

    key_blk = lax.broadcasted_iota(jnp.int32, (kc, n_blk), 0) // SEL_BLOCK
    blk_col = lax.broadcasted_iota(jnp.int32, (kc, n_blk), 1)

    n_slc = (i + 1) * (qb // kc)

    def slc_scores(c):
        cc = jnp.minimum(c, n_slc - 1)
        expand = jnp.where(blk_col == key_blk + cc * (kc // SEL_BLOCK), 1.0, 0.0).astype(BF16)
        block_bias = _dot(expand, sel_bias)
        bias = jnp.where(c * kc + sub_pos <= t_lane, block_bias, MASKED)
        k = ks_ref[0, pl.ds(pl.multiple_of(cc * kc, kc), kc), :]
        return _dot_nt(k, q) + heads(bias)

    def slc_values(c):
        cc = jnp.clip(c, 0, n_slc - 1)
        return (vst_ref[:, pl.ds(pl.multiple_of(cc * kc, kc), kc)],)

    (o_slc,) = _attend_pairs((n_slc + 1) // 2, slc_scores, slc_values, sa_ref, sb_ref, C_DIM, C_HPG * qb)

    gates_t = gt_ref[0].T
    cols = []
    for hd in range(C_HPG):
        g0, g1, g2 = (gates_t[3 * hd + j:3 * hd + j + 1, :] for j in range(3))
        blk = slice(qb * hd, qb * (hd + 1))
        cols.append(g0 * o_cmp[:, blk] + g1 * o_slc[:, blk] + g2 * o_win[:, blk])
    o_ref[...] = jnp.concatenate(cols, axis=0).T.astype(BF16)


def _nsa(q, k_cmp, v_cmp_t, ks, vst, kw, vwt, gates, batch, seq):
    n = ks.shape[1]
    nq = seq // C_QB
    n_blk = seq // SEL_BLOCK
    top_n = min(SEL_TOPN, n_blk)
    nchunk = seq // CMP_STRIDE
    k_spec = pl.BlockSpec((1, seq, C_DIM), lambda b, g, i: (g, b, 0))
    vt_spec = pl.BlockSpec((C_DIM, seq), lambda b, g, i: (g, b))
    return pl.pallas_call(
        functools.partial(_nsa_kernel, n_blk=n_blk, top_n=top_n),
        grid=(batch, C_GROUPS, nq),
        in_specs=[
            pl.BlockSpec((C_HPG, C_QB, C_DIM), lambda b, g, i: (g, b * nq + i, 0)),
            pl.BlockSpec((1, 1, nchunk, C_DIM), lambda b, g, i: (g, b, 0, 0)),
            pl.BlockSpec((1, 1, C_DIM, nchunk), lambda b, g, i: (g, b, 0, 0)),
            k_spec, vt_spec, k_spec, vt_spec,
            pl.BlockSpec((1, C_QB, LANES), lambda b, g, i: (g, b * nq + i, 0)),
        ],
        out_specs=pl.BlockSpec((C_QB, C_HPG * C_DIM), lambda b, g, i: (b * nq + i, g)),
        out_shape=jax.ShapeDtypeStruct((n, C_HEADS * C_DIM), BF16),
        scratch_shapes=[pltpu.VMEM((KEY_CHUNK, C_HPG * C_QB), F32)] * ((C_QB + WINDOW) // KEY_CHUNK),
        compiler_params=_cparams(("parallel", "parallel", "arbitrary")),
        name="nsa",
    )(q, k_cmp, v_cmp_t, ks, vst, kw, vwt, gates)


def _odd_out_kernel(x_ref, o_ref, wo_ref, y_ref):
    y_ref[...] = x_ref[...] + _dot(o_ref[...], wo_ref[...])


def _odd_out(x2, o_c, w_out):
    n = x2.shape[0]
    tm = PROJ_TM
    row = lambda i: (i, 0)
    return pl.pallas_call(
        _odd_out_kernel,
        grid=(n // tm,),
        in_specs=[
            pl.BlockSpec((tm, D_MODEL), row),
            pl.BlockSpec((tm, D_MODEL), row),
            pl.BlockSpec((D_MODEL, D_MODEL), lambda i: (0, 0)),
        ],
        out_specs=pl.BlockSpec((tm, D_MODEL), row),
        out_shape=jax.ShapeDtypeStruct((n, D_MODEL), F32),
        compiler_params=_cparams(("parallel",)),
        name="odd_out",
    )(x2, o_c, w_out)


FFN_CHUNKS = 2
FFN_TM = 512


def _ffn_kernel(x_ref, g_ref, wg_ref, wu_ref, wd_ref, gf_ref, y_ref, h_ref, acc_ref, *, final_norm):
    f = pl.program_id(1)

    @pl.when(f == 0)
    def _():
        h_ref[...] = _rms_rows(x_ref[...], g_ref[...]).astype(BF16)

    h = h_ref[...]
    gate = _dot(h, wg_ref[...])
    act = (gate * jax.nn.sigmoid(gate) * _dot(h, wu_ref[...])).astype(BF16)
    part = _dot(act, wd_ref[...])

    @pl.when(f == 0)
    def _():
        acc_ref[...] = part

    @pl.when(f == FFN_CHUNKS - 1)
    def _():
        y = x_ref[...] + acc_ref[...] + part if FFN_CHUNKS > 1 else x_ref[...] + part
        y_ref[...] = _rms_rows(y, gf_ref[...]) if final_norm else y


def _ffn(x2, gain, wg, wu, wd, g_final, final_norm):
    n = x2.shape[0]
    tm = FFN_TM
    fc = D_FF // FFN_CHUNKS
    assert FFN_CHUNKS in (1, 2) and fc * FFN_CHUNKS == D_FF and fc % LANES == 0 and n % tm == 0
    row = lambda i, f: (i, 0)
    full2 = lambda i, f: (0, 0)
    return pl.pallas_call(
        functools.partial(_ffn_kernel, final_norm=final_norm),
        grid=(n // tm, FFN_CHUNKS),
        in_specs=[
            pl.BlockSpec((tm, D_MODEL), row),
            pl.BlockSpec((1, D_MODEL), full2),
            pl.BlockSpec((D_MODEL, fc), lambda i, f: (0, f)),
            pl.BlockSpec((D_MODEL, fc), lambda i, f: (0, f)),
            pl.BlockSpec((fc, D_MODEL), lambda i, f: (f, 0)),
            pl.BlockSpec((1, D_MODEL), full2),
        ],
        out_specs=pl.BlockSpec((tm, D_MODEL), row),
        out_shape=jax.ShapeDtypeStruct((n, D_MODEL), F32),
        scratch_shapes=[pltpu.VMEM((tm, D_MODEL), BF16), pltpu.VMEM((tm, D_MODEL), F32)],
        compiler_params=_cparams(("parallel", "arbitrary")),
        name="ffn",
    )(x2, gain, wg, wu, wd, g_final)


def kernel(x, norm_mix, norm_ffn, norm_final, ev_w_in, ev_kv_gain, ev_w_uk, ev_w_uv, ev_lambda, ev_subln, ev_w_out,
           od_w_in, od_cmp_pe, od_cmp_w1, od_cmp_w2, od_w_out, ffn_w_gate, ffn_w_up, ffn_w_down):
    batch, seq, d = x.shape
    depth = norm_mix.shape[0]
    x2 = x.reshape(batch * seq, d)
    t64 = _rope_tables(seq, 64)
    t32 = _rope_tables(seq, A_ROPE)
    g_final = norm_final.reshape(1, d)
    for layer in range(depth):
        j = layer // 2
        gain = norm_mix[layer].reshape(1, d)
        if layer % 2 == 0:
            qc, kc, cvt, iq, ik, iwt, qb, kb, vbt = _even_proj(
                x2, gain, _even_weight(ev_w_in[j]), _block_diag(ev_w_uk[j]).astype(BF16),
                ev_kv_gain[j].reshape(1, A_KV_RANK), t64, t32, seq)
            o_lat = _dsa(iq, ik, iwt, qc, kc, cvt, batch, seq)
            lam_init = 0.8 - 0.6 * math.exp(-0.3 * layer)
            o_b = _diff(qb, kb, vbt, ev_lambda[j], ev_subln[j].reshape(B_V_DIM, 1), lam_init, batch, seq)
            x2 = _even_out(x2, o_lat, o_b, _block_diag(ev_w_uv[j]).astype(BF16), ev_w_out[j].astype(BF16))
        else:
            q, kc_raw, ks, kw, vc_raw, vst, vwt, gates = _odd_proj(x2, gain, _odd_weight(od_w_in[j]), t64, seq)
            k_cmp, v_cmp_t = _compress(kc_raw, vc_raw, od_cmp_pe[j], od_cmp_w1[j], od_cmp_w2[j], batch, seq)
            o_c = _nsa(q, k_cmp, v_cmp_t, ks, vst, kw, vwt, gates, batch, seq)
            x2 = _odd_out(x2, o_c, od_w_out[j].astype(BF16))
        x2 = _ffn(x2, norm_ffn[layer].reshape(1, d), ffn_w_gate[layer].astype(BF16), ffn_w_up[layer].astype(BF16),
                  ffn_w_down[layer].astype(BF16), g_final, layer == depth - 1)
    return x2.reshape(batch, seq, d)
```
